```python
import jax, jax.numpy as jnp
from jax import lax
import numpy as np

D_MODEL = 1024
BATCH = 16
SEQ = 2048
DEPTH = 2
DEC_BATCH = 128
DEC_SEQ = 4
PAST_LEN = 16384
PAGE_SIZE = 128

N_MIXERS = 2
N_CONV_LAYERS = (DEPTH + 1) // 2
N_MLA_LAYERS = DEPTH // 2
CONV_WIDTH = 3
D_FF = 2816
N_HEADS = 8
Q_LORA = 384
KV_LORA = 256
NOPE_DIM = 128
ROPE_DIM = 64
V_DIM = 128
QK_DIM = NOPE_DIM + ROPE_DIM
ROPE_THETA = 10000.0
Q_BLOCK = 128
EPS = 1e-6
SOFTMAX_SCALE = QK_DIM ** -0.5
N_SUB = 3

kernel_name = 'macaron_conv_mla_adaln_step'


def rmsnorm(x, g):
    xf = x.astype(jnp.float32)
    y = xf * lax.rsqrt(jnp.mean(xf * xf, axis=-1, keepdims=True) + EPS)
    return (y * g.astype(jnp.float32)).astype(x.dtype)


def adaln(c, w, b):
    m = jax.nn.silu(c) @ w + b
    return m.reshape(c.shape[0], N_SUB, 3, c.shape[-1])


def modulated(x, g, mod, s):
    h = rmsnorm(x, g)
    return h * (1 + mod[:, s, 1][:, None]) + mod[:, s, 0][:, None]


def swiglu(h, w_gu, w_down):
    gate, up = jnp.split(h @ w_gu, 2, axis=-1)
    return (jax.nn.silu(gate) * up) @ w_down


def half_ffn(x, g, mod, s, w_gu, w_down):
    return x + 0.5 * mod[:, s, 2][:, None] * swiglu(modulated(x, g, mod, s), w_gu, w_down)


def short_conv(h, conv_state, w_in, w_k, w_out):
    T = h.shape[1]
    b_gate, c_gate, v = jnp.split(h @ w_in, 3, axis=-1)
    u = c_gate * v
    u_ext = jnp.concatenate([conv_state, u], axis=1)
    conv = sum(w_k[k] * u_ext[:, k:k + T] for k in range(CONV_WIDTH))
    y = (b_gate * conv) @ w_out
    return y, u_ext[:, -(CONV_WIDTH - 1):]


def rope_tables(pos, dtype):
    half = ROPE_DIM // 2
    inv = ROPE_THETA ** (-jnp.arange(half, dtype=jnp.float32) * (2.0 / ROPE_DIM))
    ang = pos.astype(jnp.float32)[:, None] * inv[None, :]
    return jnp.cos(ang).astype(dtype), jnp.sin(ang).astype(dtype)


def apply_rope(x, cos, sin):
    half = ROPE_DIM // 2
    x1, x2 = x[..., :half], x[..., half:]
    return jnp.concatenate([x1 * cos - x2 * sin, x1 * sin + x2 * cos], axis=-1)


def mla_project(h, pos, w_in, g_q, g_kv, w_uq):
    d = h @ w_in
    cq = rmsnorm(d[..., :Q_LORA], g_q)
    ckv = rmsnorm(d[..., Q_LORA:Q_LORA + KV_LORA], g_kv)
    kr = d[..., Q_LORA + KV_LORA:]
    q = jnp.einsum('btr,rhe->bthe', cq, w_uq)
    cos, sin = rope_tables(pos, h.dtype)
    q_nope = q[..., :NOPE_DIM]
    q_rope = apply_rope(q[..., NOPE_DIM:], cos[:, None], sin[:, None])
    kr = apply_rope(kr, cos, sin)
    return q_nope, q_rope, ckv, kr


def mla_prompt(h, w_in, g_q, g_kv, w_uq, w_uk, w_uv, w_o):
    B, S, _ = h.shape
    pos = jnp.arange(S, dtype=jnp.int32)
    q_nope, q_rope, ckv, kr = mla_project(h, pos, w_in, g_q, g_kv, w_uq)
    k_nope = jnp.einsum('bsr,rhe->bshe', ckv, w_uk)
    v = jnp.einsum('bsr,rhe->bshe', ckv, w_uv)
    key_pos = jnp.arange(S)

    def block(i):
        start = i * Q_BLOCK
        qn = lax.dynamic_slice_in_dim(q_nope, start, Q_BLOCK, axis=1)
        qr = lax.dynamic_slice_in_dim(q_rope, start, Q_BLOCK, axis=1)
        s = jnp.einsum('bqhe,bkhe->bhqk', qn, k_nope) + jnp.einsum('bqhe,bke->bhqk', qr, kr)
        s = s.astype(jnp.float32) * SOFTMAX_SCALE
        causal = (start + jnp.arange(Q_BLOCK))[:, None] >= key_pos[None, :]
        p = jax.nn.softmax(jnp.where(causal, s, -jnp.inf), axis=-1).astype(v.dtype)
        return jnp.einsum('bhqk,bkhe->bqhe', p, v)

    o = lax.map(block, jnp.arange(S // Q_BLOCK))
    o = jnp.moveaxis(o, 0, 1).reshape(B, S, N_HEADS * V_DIM)
    return o @ w_o, ckv, kr


def mla_sample(h, lat_past, kr_past, w_in, g_q, g_kv, w_uq, w_uk, w_uv, w_o):
    B, T, _ = h.shape
    past = lat_past.shape[1]
    pos = past + jnp.arange(T, dtype=jnp.int32)
    q_nope, q_rope, ckv, kr = mla_project(h, pos, w_in, g_q, g_kv, w_uq)
    q_lat = jnp.einsum('bthe,rhe->bthr', q_nope, w_uk)
    s_past = jnp.einsum('bthr,bsr->bths', q_lat, lat_past) + jnp.einsum('bthe,bse->bths', q_rope, kr_past)
    s_new = jnp.einsum('bthr,bur->bthu', q_lat, ckv) + jnp.einsum('bthe,bue->bthu', q_rope, kr)
    causal = jnp.arange(T)[:, None] >= jnp.arange(T)[None, :]
    s_new = jnp.where(causal[:, None, :], s_new.astype(jnp.float32) * SOFTMAX_SCALE, -jnp.inf)
    s = jnp.concatenate([s_past.astype(jnp.float32) * SOFTMAX_SCALE, s_new], axis=-1)
    p = jax.nn.softmax(s, axis=-1).astype(h.dtype)
    o_lat = (jnp.einsum('bths,bsr->bthr', p[..., :past], lat_past)
             + jnp.einsum('bthu,bur->bthr', p[..., past:], ckv))
    o = jnp.einsum('bthr,rhe->bthe', o_lat, w_uv).reshape(B, T, N_HEADS * V_DIM)
    return o @ w_o, ckv, kr


def setup_inputs(seed: int = 0) -> dict:
    key = jax.random.key(seed)
    ks = jax.random.split(key, 32)
    f32 = jnp.float32
    D = D_MODEL
    n_pages = PAST_LEN // PAGE_SIZE
    n_used = DEC_BATCH * n_pages
    n_pool = n_used + n_used // 4
    nrm = lambda k, shape, scale: jax.random.normal(k, shape, f32) * scale
    page_table = jax.random.permutation(ks[0], n_pool)[:n_used].reshape(DEC_BATCH, n_pages).astype(jnp.int32)
    return {
        'x_prompt': nrm(ks[1], (BATCH, SEQ, D), 1.0),
        'x_sample': nrm(ks[2], (DEC_BATCH, DEC_SEQ, D), 1.0),
        'c_prompt': nrm(ks[3], (BATCH, D), 1.0),
        'c_sample': nrm(ks[4], (DEC_BATCH, D), 1.0),
        'state_conv': nrm(ks[5], (N_CONV_LAYERS, DEC_BATCH, CONV_WIDTH - 1, D), 1.0),
        'cache_kv_latent': nrm(ks[6], (N_MLA_LAYERS, n_pool, PAGE_SIZE, KV_LORA), 1.0),
        'cache_k_rope': nrm(ks[7], (N_MLA_LAYERS, n_pool, PAGE_SIZE, ROPE_DIM), 1.0),
        'page_table': page_table,
        'w_ada': nrm(ks[8], (DEPTH, D, N_SUB * 3 * D), 0.5 * D ** -0.5),
        'b_ada': nrm(ks[9], (DEPTH, N_SUB * 3 * D), 0.02),
        'g_norm': 1.0 + nrm(ks[10], (DEPTH, N_SUB, D), 0.05),
        'w_ffn_gu': nrm(ks[11], (DEPTH, 2, D, 2 * D_FF), D ** -0.5),
        'w_ffn_down': nrm(ks[12], (DEPTH, 2, D_FF, D), D_FF ** -0.5),
        'w_conv_in': nrm(ks[13], (N_CONV_LAYERS, D, 3 * D), D ** -0.5),
        'w_conv_k': nrm(ks[14], (N_CONV_LAYERS, CONV_WIDTH, D), CONV_WIDTH ** -0.5),
        'w_conv_out': nrm(ks[15], (N_CONV_LAYERS, D, D), D ** -0.5),
        'w_mla_in': nrm(ks[16], (N_MLA_LAYERS, D, Q_LORA + KV_LORA + ROPE_DIM), D ** -0.5),
        'g_q_norm': 1.0 + nrm(ks[17], (N_MLA_LAYERS, Q_LORA), 0.05),
        'g_kv_norm': 1.0 + nrm(ks[18], (N_MLA_LAYERS, KV_LORA), 0.05),
        'w_uq': nrm(ks[19], (N_MLA_LAYERS, Q_LORA, N_HEADS, QK_DIM), Q_LORA ** -0.5),
        'w_uk': nrm(ks[20], (N_MLA_LAYERS, KV_LORA, N_HEADS, NOPE_DIM), KV_LORA ** -0.5),
        'w_uv': nrm(ks[21], (N_MLA_LAYERS, KV_LORA, N_HEADS, V_DIM), KV_LORA ** -0.5),
        'w_mla_out': nrm(ks[22], (N_MLA_LAYERS, N_HEADS * V_DIM, D), (N_HEADS * V_DIM) ** -0.5),
        'g_final': 1.0 + nrm(ks[23], (D,), 0.05),
    }


def reference(x_prompt, x_sample, c_prompt, c_sample, state_conv, cache_kv_latent, cache_k_rope, page_table,
              w_ada, b_ada, g_norm, w_ffn_gu, w_ffn_down, w_conv_in, w_conv_k, w_conv_out,
              w_mla_in, g_q_norm, g_kv_norm, w_uq, w_uk, w_uv, w_mla_out, g_final):
    xp, xs = x_prompt, x_sample
    dec_b = x_sample.shape[0]
    past = page_table.shape[1] * cache_kv_latent.shape[2]
    conv_p, conv_s, lat_p, kr_p, lat_s, kr_s = [], [], [], [], [], []
    for i in range(DEPTH):
        mp = adaln(c_prompt, w_ada[i], b_ada[i])
        ms = adaln(c_sample, w_ada[i], b_ada[i])
        xp = half_ffn(xp, g_norm[i, 0], mp, 0, w_ffn_gu[i, 0], w_ffn_down[i, 0])
        xs = half_ffn(xs, g_norm[i, 0], ms, 0, w_ffn_gu[i, 0], w_ffn_down[i, 0])
        hp = modulated(xp, g_norm[i, 1], mp, 1)
        hs = modulated(xs, g_norm[i, 1], ms, 1)
        j = i // N_MIXERS
        if i % N_MIXERS == 0:
            zero_state = jnp.zeros((hp.shape[0], CONV_WIDTH - 1, hp.shape[-1]), hp.dtype)
            yp, st_p = short_conv(hp, zero_state, w_conv_in[j], w_conv_k[j], w_conv_out[j])
            ys, st_s = short_conv(hs, state_conv[j], w_conv_in[j], w_conv_k[j], w_conv_out[j])
            conv_p.append(st_p)
            conv_s.append(st_s)
        else:
            yp, ckv_p, k_p = mla_prompt(hp, w_mla_in[j], g_q_norm[j], g_kv_norm[j], w_uq[j], w_uk[j], w_uv[j], w_mla_out[j])
            lat_past = cache_kv_latent[j, page_table].reshape(dec_b, past, KV_LORA)
            kr_past = cache_k_rope[j, page_table].reshape(dec_b, past, ROPE_DIM)
            ys, ckv_s, k_s = mla_sample(hs, lat_past, kr_past, w_mla_in[j], g_q_norm[j], g_kv_norm[j], w_uq[j], w_uk[j], w_uv[j], w_mla_out[j])
            lat_p.append(ckv_p)
            kr_p.append(k_p)
            lat_s.append(ckv_s)
            kr_s.append(k_s)
        xp = xp + mp[:, 1, 2][:, None] * yp
        xs = xs + ms[:, 1, 2][:, None] * ys
        xp = half_ffn(xp, g_norm[i, 2], mp, 2, w_ffn_gu[i, 1], w_ffn_down[i, 1])
        xs = half_ffn(xs, g_norm[i, 2], ms, 2, w_ffn_gu[i, 1], w_ffn_down[i, 1])
    y_prompt = rmsnorm(xp, g_final)
    y_sample = rmsnorm(xs, g_final)
    return (y_prompt, y_sample, jnp.stack(conv_p), jnp.stack(conv_s), jnp.stack(lat_p), jnp.stack(kr_p), jnp.stack(lat_s), jnp.stack(kr_s))
```

```python
import functools

import jax
import jax.numpy as jnp
from jax import lax
from jax.experimental import pallas as pl
from jax.experimental.pallas import tpu as pltpu

EPS = 1e-6
ROPE_THETA = 10000.0
N_SUB = 3
N_MOD = 3 * N_SUB
N_MIXERS = 2

BF16 = jnp.bfloat16
F32 = jnp.float32

LANES = 128
VMEM_LIMIT_BYTES = 56 * 1024 * 1024
NEG_BIG = -1e30


def _params(*semantics):
    return pltpu.CompilerParams(dimension_semantics=semantics, vmem_limit_bytes=VMEM_LIMIT_BYTES)


def _dot(a, b):
    return jnp.dot(a, b, preferred_element_type=F32)


def _dot_nt(a, b):
    return lax.dot_general(a, b, (((1,), (1,)), ((), ())), preferred_element_type=F32)


def _rmsnorm(x, g):
    return x * lax.rsqrt(jnp.mean(x * x, axis=-1, keepdims=True) + EPS) * g


def _modulated(x, g, mod_ref, s):
    return _rmsnorm(x, g) * (1.0 + mod_ref[3 * s + 1]) + mod_ref[3 * s]


def _tile(n, want):
    t = min(n, want)
    while n % t:
        t -= 1
    return t


def _mod_spec(mod4, rows, tm):
    d = mod4.shape[-1]
    if mod4.shape[2] == 1:
        rows_per_seq = rows // mod4.shape[0]
        assert rows_per_seq % tm == 0
        tiles_per_seq = rows_per_seq // tm
        return pl.BlockSpec((None, N_MOD, 1, d), lambda i, *_: (i // tiles_per_seq, 0, 0, 0))
    assert mod4.shape[2] == rows
    return pl.BlockSpec((None, N_MOD, tm, d), lambda i, *_: (0, 0, i, 0))


def _adaln_kernel(c_ref, w_ref, b_ref, o_ref):
    c = c_ref[...]
    a = (c * jax.nn.sigmoid(c)).astype(BF16)
    o_ref[...] = _dot(a, w_ref[...].astype(BF16)) + b_ref[...]


def _adaln(c_all, w_ada, b_ada):
    depth, d, n_out = w_ada.shape
    n_seq = c_all.shape[0]
    tn = _tile(n_out, 1024)
    return pl.pallas_call(
        _adaln_kernel,
        grid=(depth, n_out // tn),
        in_specs=[
            pl.BlockSpec((n_seq, d), lambda l, j: (0, 0)),
            pl.BlockSpec((None, d, tn), lambda l, j: (l, 0, j)),
            pl.BlockSpec((None, 1, tn), lambda l, j: (l, 0, j)),
        ],
        out_specs=pl.BlockSpec((None, n_seq, tn), lambda l, j: (l, 0, j)),
        out_shape=jax.ShapeDtypeStruct((depth, n_seq, n_out), F32),
        compiler_params=_params("arbitrary", "arbitrary"),
    )(c_all, w_ada, b_ada.reshape(depth, 1, n_out))


def _ffn_kernel(*refs, s, n_f, final_norm):
    if final_norm:
        x_ref, mod_ref, g_ref, wg_ref, wu_ref, wd_ref, gf_ref, o_ref, h_ref = refs
    else:
        x_ref, mod_ref, g_ref, wg_ref, wu_ref, wd_ref, o_ref, h_ref = refs
    j = pl.program_id(1)

    @pl.when(j == 0)
    def _():
        h_ref[...] = _modulated(x_ref[...], g_ref[...], mod_ref, s).astype(BF16)

    h = h_ref[...]
    gate = _dot(h, wg_ref[...])
    up = _dot(h, wu_ref[...])
    a = (gate * jax.nn.sigmoid(gate) * up).astype(BF16)
    part = _dot(a, wd_ref[...])

    @pl.when(j == 0)
    def _():
        o_ref[...] = part

    @pl.when(j > 0)
    def _():
        o_ref[...] += part

    @pl.when(j == n_f - 1)
    def _():
        xn = x_ref[...] + 0.5 * mod_ref[3 * s + 2] * o_ref[...]
        if final_norm:
            xn = _rmsnorm(xn, gf_ref[...])
        o_ref[...] = xn


def _ffn(x, mod4, g_norm4, w_gu, w_down, layer, k, s, g_final=None):
    rows, d = x.shape
    f = w_down.shape[2]
    tm = _tile(rows if mod4.shape[2] != 1 else rows // mod4.shape[0], 1024)
    tf = _tile(f, 256)
    n_f = f // tf
    in_specs = [
        pl.BlockSpec((tm, d), lambda i, j: (i, 0)),
        _mod_spec(mod4, rows, tm),
        pl.BlockSpec((None, None, 1, d), lambda i, j: (layer, s, 0, 0)),
        pl.BlockSpec((None, None, d, tf), lambda i, j: (layer, k, 0, j)),
        pl.BlockSpec((None, None, d, tf), lambda i, j: (layer, k, 0, j + n_f)),
        pl.BlockSpec((None, None, tf, d), lambda i, j: (layer, k, j, 0)),
    ]
    args = [x, mod4, g_norm4, w_gu, w_gu, w_down]
    if g_final is not None:
        in_specs.append(pl.BlockSpec((1, d), lambda i, j: (0, 0)))
        args.append(g_final.reshape(1, d))
    return pl.pallas_call(
        functools.partial(_ffn_kernel, s=s, n_f=n_f, final_norm=g_final is not None),
        grid=(rows // tm, n_f),
        in_specs=in_specs,
        out_specs=pl.BlockSpec((tm, d), lambda i, j: (i, 0)),
        out_shape=jax.ShapeDtypeStruct((rows, d), F32),
        scratch_shapes=[pltpu.VMEM((tm, d), BF16)],
        compiler_params=_params("arbitrary", "arbitrary"),
    )(*args)


def _conv_taps(u, u1, u2, wk_ref):
    return wk_ref[0:1, :] * u2 + wk_ref[1:2, :] * u1 + wk_ref[2:3, :] * u


def _conv_gates(x_ref, mod_ref, g_ref, win_ref):
    d = x_ref.shape[-1]
    h = _modulated(x_ref[...], g_ref[...], mod_ref, 1).astype(BF16)
    p = _dot(h, win_ref[...])
    return p[:, :d], p[:, d:2 * d] * p[:, 2 * d:]


def _conv_finish(x_ref, mod_ref, b_gate, conv, wout_ref, o_ref):
    y = _dot((b_gate * conv).astype(BF16), wout_ref[...])
    o_ref[...] = x_ref[...] + mod_ref[5] * y


def _conv_seq_kernel(x_ref, mod_ref, g_ref, win_ref, wk_ref, wout_ref, o_ref, st_ref, carry_ref, *, tiles_per_seq):
    tm = x_ref.shape[0]

    @pl.when(pl.program_id(0) % tiles_per_seq == 0)
    def _():
        carry_ref[...] = jnp.zeros_like(carry_ref)

    b_gate, u = _conv_gates(x_ref, mod_ref, g_ref, win_ref)
    row = lax.broadcasted_iota(jnp.int32, (tm, 1), 0)
    prev0 = carry_ref[0:1, :]
    prev1 = carry_ref[1:2, :]
    u1 = jnp.where(row == 0, prev1, pltpu.roll(u, 1, axis=0))
    u2 = jnp.where(row == 0, prev0, jnp.where(row == 1, prev1, pltpu.roll(u, 2, axis=0)))
    _conv_finish(x_ref, mod_ref, b_gate, _conv_taps(u, u1, u2, wk_ref), wout_ref, o_ref)
    carry_ref[...] = u[tm - 2:, :]
    st_ref[...] = u[tm - 2:, :]


def _conv_step_kernel(x_ref, mod_ref, g_ref, win_ref, wk_ref, wout_ref, p1_ref, p2_ref, o_ref, u_ref, *, seq):
    tm = x_ref.shape[0]
    b_gate, u = _conv_gates(x_ref, mod_ref, g_ref, win_ref)
    t = lax.broadcasted_iota(jnp.int32, (tm, 1), 0) % seq
    u1 = jnp.where(t >= 1, pltpu.roll(u, 1, axis=0), p1_ref[...])
    u2 = jnp.where(t >= 2, pltpu.roll(u, 2, axis=0), p2_ref[...])
    _conv_finish(x_ref, mod_ref, b_gate, _conv_taps(u, u1, u2, wk_ref), wout_ref, o_ref)
    u_ref[...] = u


def _conv_weight_specs(d, n_taps, layer, j):
    return [
        pl.BlockSpec((None, None, 1, d), lambda i: (layer, 1, 0, 0)),
        pl.BlockSpec((None, d, 3 * d), lambda i: (j, 0, 0)),
        pl.BlockSpec((None, n_taps, d), lambda i: (j, 0, 0)),
        pl.BlockSpec((None, d, d), lambda i: (j, 0, 0)),
    ]


def _conv_prompt(x, mod4, g_norm4, w_in, w_k, w_out, layer, j, n_seq):
    rows, d = x.shape
    seq = rows // n_seq
    assert w_k.shape[1] == 3 and seq >= 2
    tm = _tile(seq, 512)
    assert tm >= 2
    tiles_per_seq = seq // tm
    return pl.pallas_call(
        functools.partial(_conv_seq_kernel, tiles_per_seq=tiles_per_seq),
        grid=(rows // tm,),
        in_specs=[pl.BlockSpec((tm, d), lambda i: (i, 0)), _mod_spec(mod4, rows, tm)]
        + _conv_weight_specs(d, 3, layer, j),
        out_specs=[
            pl.BlockSpec((tm, d), lambda i: (i, 0)),
            pl.BlockSpec((None, 2, d), lambda i: (i // tiles_per_seq, 0, 0)),
        ],
        out_shape=[jax.ShapeDtypeStruct((rows, d), F32), jax.ShapeDtypeStruct((n_seq, 2, d), F32)],
        scratch_shapes=[pltpu.VMEM((2, d), F32)],
        compiler_params=_params("arbitrary"),
    )(x, mod4, g_norm4, w_in, w_k, w_out)


def _conv_sample(x, mod4, g_norm4, w_in, w_k, w_out, state, layer, j):
    rows, d = x.shape
    n_seq = state.shape[0]
    seq = rows // n_seq
    assert w_k.shape[1] == 3 and seq >= 2
    zeros = lambda n: jnp.zeros((n_seq, n, d), F32)
    prev1 = jnp.concatenate([state[:, 1:2], zeros(seq - 1)], axis=1).reshape(rows, d)
    prev2 = jnp.concatenate([state, zeros(seq - 2)], axis=1).reshape(rows, d)
    seqs_per_tile = _tile(n_seq, max(1, 512 // seq))
    tm = seqs_per_tile * seq
    row_spec = pl.BlockSpec((tm, d), lambda i: (i, 0))
    x_new, u = pl.pallas_call(
        functools.partial(_conv_step_kernel, seq=seq),
        grid=(rows // tm,),
        in_specs=[row_spec, _mod_spec(mod4, rows, tm)] + _conv_weight_specs(d, 3, layer, j) + [row_spec, row_spec],
        out_specs=[row_spec, row_spec],
        out_shape=[jax.ShapeDtypeStruct((rows, d), F32)] * 2,
        compiler_params=_params("arbitrary"),
    )(x, mod4, g_norm4, w_in, w_k, w_out, prev1, prev2)
    return x_new, u.reshape(n_seq, seq, d)[:, seq - 2:]


def _mla_proj_kernel(*refs, dims, absorb):
    q_lora, kv_lora, rope, n_heads, nope, rw = dims
    if absorb:
        (x_ref, mod_ref, g_ref, win_ref, gq_ref, gkv_ref, wuq_ref, cq_ref, sq_ref, ck_ref, sk_ref, wukt_ref,
         ql_ref, qr_ref, ckv_ref, kr_ref) = refs
    else:
        (x_ref, mod_ref, g_ref, win_ref, gq_ref, gkv_ref, wuq_ref, cq_ref, sq_ref, ck_ref, sk_ref, wuk_ref, wuv_ref,
         qn_ref, qr_ref, kn_ref, v_ref, krp_ref, ckv_ref, kr_ref) = refs
    h = _modulated(x_ref[...], g_ref[...], mod_ref, 1).astype(BF16)
    dn = _dot(h, win_ref[...])
    cq = _rmsnorm(dn[:, :q_lora], gq_ref[...]).astype(BF16)
    ckv = _rmsnorm(dn[:, q_lora:q_lora + kv_lora], gkv_ref[...])
    ckv_ref[...] = ckv
    o = q_lora + kv_lora
    krp = dn[:, o:o + LANES] * ck_ref[...] + dn[:, o + LANES:o + 2 * LANES] * sk_ref[...]
    kr_ref[...] = krp[:, :rope]
    q = _dot(cq, wuq_ref[...])
    hn = n_heads * nope
    hr = n_heads * rw
    qr_ref[...] = (q[:, hn:hn + hr] * cq_ref[...] + q[:, hn + hr:hn + 2 * hr] * sq_ref[...]).astype(BF16)
    if absorb:
        for i in range(n_heads):
            qn_i = q[:, i * nope:(i + 1) * nope].astype(BF16)
            ql_ref[:, i * kv_lora:(i + 1) * kv_lora] = _dot(qn_i, wukt_ref[i]).astype(BF16)
    else:
        qn_ref[...] = q[:, :hn].astype(BF16)
        ckv_b = ckv.astype(BF16)
        kn_ref[...] = _dot(ckv_b, wuk_ref[...]).astype(BF16)
        v_ref[...] = _dot(ckv_b, wuv_ref[...]).astype(BF16)
        krp_ref[...] = krp.astype(BF16)


def _rope_tables(pos, rope, width, reps):
    half = rope // 2
    inv = ROPE_THETA ** (-jnp.arange(half, dtype=F32) * (2.0 / rope))
    ang = pos.astype(F32)[:, None] * inv[None, :]
    cos, sin = jnp.cos(ang), jnp.sin(ang)
    pad = jnp.zeros((pos.shape[0], width - rope), F32)
    c = jnp.concatenate([cos, cos, pad], axis=1)
    s = jnp.concatenate([-sin, sin, pad], axis=1)
    return jnp.tile(c, (1, reps)), jnp.tile(s, (1, reps))


def _swap_halves(w, rope):
    half = rope // 2
    return jnp.concatenate([w[..., half:], w[..., :half]], axis=-1)


def _pad_last(w, width):
    return jnp.pad(w, [(0, 0)] * (w.ndim - 1) + [(0, width - w.shape[-1])])


def _mla_weights(w_in, w_uq, q_lora, kv_lora, rope, nope, rw):
    d = w_in.shape[0]
    n_heads = w_uq.shape[1]
    w_kr = w_in[:, q_lora + kv_lora:]
    w_in_ext = jnp.concatenate(
        [w_in[:, :q_lora + kv_lora], _pad_last(w_kr, LANES), _pad_last(_swap_halves(w_kr, rope), LANES)], axis=1)
    w_qr = w_uq[:, :, nope:]
    w_uq_ext = jnp.concatenate([
        w_uq[:, :, :nope].reshape(q_lora, n_heads * nope),
        _pad_last(w_qr, rw).reshape(q_lora, n_heads * rw),
        _pad_last(_swap_halves(w_qr, rope), rw).reshape(q_lora, n_heads * rw)], axis=1)
    assert w_in_ext.shape == (d, q_lora + kv_lora + 2 * LANES)
    return w_in_ext.astype(BF16), w_uq_ext.astype(BF16)


def _mla_project(x, mod4, g_norm4, layer, w_in, g_q, g_kv, w_uq, w_uk, w_uv, pos_rows, rows_per_pos_table, absorb):
    rows, d = x.shape
    q_lora, kv_lora = g_q.shape[0], g_kv.shape[0]
    rope = w_in.shape[1] - q_lora - kv_lora
    n_heads, nope = w_uk.shape[1], w_uk.shape[2]
    v_dim = w_uv.shape[2]
    rw = rope if absorb else LANES
    w_in_ext, w_uq_ext = _mla_weights(w_in, w_uq, q_lora, kv_lora, rope, nope, rw)
    cos_q, sin_q = _rope_tables(pos_rows, rope, rw, n_heads)
    cos_k, sin_k = _rope_tables(pos_rows, rope, LANES, 1)
    period = rows_per_pos_table
    tm = _tile(period, 512)
    tiles_per_period = period // tm
    row = lambda w: pl.BlockSpec((tm, w), lambda i: (i, 0))
    tab = lambda w: pl.BlockSpec((tm, w), lambda i: (i % tiles_per_period, 0))
    full = lambda a: pl.BlockSpec(a.shape, lambda i: (0,) * a.ndim)
    hn, hr = n_heads * nope, n_heads * rw
    in_specs = [
        row(d), _mod_spec(mod4, rows, tm),
        pl.BlockSpec((None, None, 1, d), lambda i: (layer, 1, 0, 0)),
        full(w_in_ext), pl.BlockSpec((1, q_lora), lambda i: (0, 0)), pl.BlockSpec((1, kv_lora), lambda i: (0, 0)),
        full(w_uq_ext), tab(hr), tab(hr), tab(LANES), tab(LANES),
    ]
    args = [x, mod4, g_norm4, w_in_ext, g_q.reshape(1, q_lora), g_kv.reshape(1, kv_lora), w_uq_ext,
            cos_q, sin_q, cos_k, sin_k]
    if absorb:
        w_ukt = jnp.transpose(w_uk, (1, 2, 0)).astype(BF16)
        in_specs.append(full(w_ukt))
        args.append(w_ukt)
        out_widths = [(n_heads * kv_lora, BF16), (hr, BF16), (kv_lora, F32), (rope, F32)]
    else:
        w_uk2 = w_uk.reshape(kv_lora, hn).astype(BF16)
        w_uv2 = w_uv.reshape(kv_lora, n_heads * v_dim).astype(BF16)
        in_specs += [full(w_uk2), full(w_uv2)]
        args += [w_uk2, w_uv2]
        out_widths = [(hn, BF16), (hr, BF16), (hn, BF16), (n_heads * v_dim, BF16), (LANES, BF16),
                      (kv_lora, F32), (rope, F32)]
    return pl.pallas_call(
        functools.partial(_mla_proj_kernel, dims=(q_lora, kv_lora, rope, n_heads, nope, rw), absorb=absorb),
        grid=(rows // tm,),
        in_specs=in_specs,
        out_specs=[row(w) for w, _ in out_widths],
        out_shape=[jax.ShapeDtypeStruct((rows, w), dt) for w, dt in out_widths],
        compiler_params=_params("arbitrary"),
    )(*args)


def _attn_kernel(qn_ref, qr_ref, kn_ref, v_ref, kr_ref, x_ref, mod_ref, wo_ref, o_ref, oh_ref,
                 *, n_heads, nope, v_dim, tk, scale):
    tq = qn_ref.shape[0]
    q0 = pl.program_id(1) * tq
    n_kv = (q0 + tq + tk - 1) // tk
    row = q0 + lax.broadcasted_iota(jnp.int32, (tq, 1), 0)
    col0 = lax.broadcasted_iota(jnp.int32, (1, tk), 1)
    for i in range(n_heads):
        qn = qn_ref[:, i * nope:(i + 1) * nope]
        qr = qr_ref[:, i * LANES:(i + 1) * LANES]

        def step(c, carry, i=i, qn=qn, qr=qr):
            m, l, acc = carry
            off = pl.multiple_of(c * tk, tk)
            s = _dot_nt(qn, kn_ref[pl.ds(off, tk), i * nope:(i + 1) * nope]) + _dot_nt(qr, kr_ref[pl.ds(off, tk), :])
            s = jnp.where(row >= off + col0, s * scale, -jnp.inf)
            m_new = jnp.maximum(m, jnp.max(s, axis=-1, keepdims=True))
            alpha = jnp.exp(m - m_new)
            p = jnp.exp(s - m_new)
            l = alpha * l + jnp.sum(p, axis=-1, keepdims=True)
            acc = alpha * acc + _dot(p.astype(BF16), v_ref[pl.ds(off, tk), i * v_dim:(i + 1) * v_dim])
            return m_new, l, acc

        init = (jnp.full((tq, 1), NEG_BIG, F32), jnp.zeros((tq, 1), F32), jnp.zeros((tq, v_dim), F32))
        _, l, acc = lax.fori_loop(0, n_kv, step, init)
        oh_ref[:, i * v_dim:(i + 1) * v_dim] = (acc / l).astype(BF16)
    o_ref[...] = x_ref[...] + mod_ref[5] * _dot(oh_ref[...], wo_ref[...])


def _attn_prompt(qn, qr, kn, v, krp, x, mod4, w_o, j, n_seq, n_heads, nope, v_dim, scale):
    rows, d = x.shape
    seq = rows // n_seq
    tq = _tile(seq, 512)
    tk = tq
    r3 = lambda a: a.reshape(n_seq, seq, a.shape[-1])
    q_spec = lambda w: pl.BlockSpec((None, tq, w), lambda b, i: (b, i, 0))
    kv_spec = lambda w: pl.BlockSpec((None, seq, w), lambda b, i: (b, 0, 0))
    out = pl.pallas_call(
        functools.partial(_attn_kernel, n_heads=n_heads, nope=nope, v_dim=v_dim, tk=tk, scale=scale),
        grid=(n_seq, seq // tq),
        in_specs=[
            q_spec(qn.shape[-1]), q_spec(qr.shape[-1]), kv_spec(kn.shape[-1]), kv_spec(v.shape[-1]),
            kv_spec(krp.shape[-1]), q_spec(d),
            pl.BlockSpec((None, N_MOD, 1, d), lambda b, i: (b, 0, 0, 0)),
            pl.BlockSpec((None, n_heads * v_dim, d), lambda b, i: (j, 0, 0)),
        ],
        out_specs=q_spec(d),
        out_shape=jax.ShapeDtypeStruct((n_seq, seq, d), F32),
        scratch_shapes=[pltpu.VMEM((tq, n_heads * v_dim), BF16)],
        compiler_params=_params("arbitrary", "arbitrary"),
    )(r3(qn), r3(qr), r3(kn), r3(v), r3(krp), r3(x), mod4, w_o)
    return out.reshape(rows, d)


def _decode_kernel(pt_ref, ql_ref, qr_ref, cn_ref, kn_ref, lat_hbm, krt_hbm, o_ref, lat_buf, krt_buf, sem,
                   *, layer, n_chunks, pages_per_chunk, page, n_heads, n_new, scale):
    b = pl.program_id(0)
    n_b = pl.num_programs(0)
    n_rows = ql_ref.shape[0]

    def copies(bb, c, slot):
        out = []
        for k in range(pages_per_chunk):
            pg = pt_ref[bb, c * pages_per_chunk + k]
            dst = pl.ds(k * page, page)
            out.append(pltpu.make_async_copy(lat_hbm.at[layer, pg], lat_buf.at[slot, dst], sem.at[0, slot]))
            out.append(pltpu.make_async_copy(krt_hbm.at[layer, pg], krt_buf.at[slot, :, dst], sem.at[1, slot]))
        return out

    def start(bb, c, slot):
        for cp in copies(bb, c, slot):
            cp.start()

    @pl.when(b == 0)
    def _():
        start(0, 0, 0)

    ql = ql_ref[...]
    qr = qr_ref[...]

    def online(carry, s, values):
        m, l, acc = carry
        m_new = jnp.maximum(m, jnp.max(s, axis=-1, keepdims=True))
        alpha = jnp.exp(m - m_new)
        p = jnp.exp(s - m_new)
        l = alpha * l + jnp.sum(p, axis=-1, keepdims=True)
        acc = alpha * acc + _dot(p.astype(BF16), values)
        return m_new, l, acc

    def step(c, carry):
        slot = (b * n_chunks + c) % 2
        last = c + 1 == n_chunks
        nb = jnp.where(last, b + 1, b)
        nc = jnp.where(last, 0, c + 1)

        @pl.when(nb < n_b)
        def _():
            start(nb, nc, 1 - slot)

        for cp in copies(b, c, slot):
            cp.wait()
        lat = lat_buf[slot].astype(BF16)
        krt = krt_buf[slot].astype(BF16)
        s = (_dot_nt(ql, lat) + _dot(qr, krt)) * scale
        return online(carry, s, lat)

    init = (jnp.full((n_rows, 1), NEG_BIG, F32), jnp.zeros((n_rows, 1), F32),
            jnp.zeros((n_rows, lat_buf.shape[-1]), F32))
    carry = lax.fori_loop(0, n_chunks, step, init)

    cn = cn_ref[...]
    s_new = (_dot_nt(ql, cn) + _dot_nt(qr, kn_ref[...])) * scale
    t = lax.broadcasted_iota(jnp.int32, (n_rows, 1), 0) // n_heads
    u = lax.broadcasted_iota(jnp.int32, (1, cn.shape[0]), 1)
    s_new = jnp.where((u <= t) & (u < n_new), s_new, -jnp.inf)
    _, l, acc = online(carry, s_new, cn)
    o_ref[...] = acc / l


def _decode_attn(page_table, q_lat, q_rope, ckv_new, kr_new, cache_lat, cache_krt, layer, n_heads, scale):
    n_b, n_rows, kv_lora = q_lat.shape
    rope = q_rope.shape[-1]
    n_new = n_rows // n_heads
    page = cache_lat.shape[2]
    n_pages = page_table.shape[1]
    pages_per_chunk = _tile(n_pages, 16)
    n_chunks = n_pages // pages_per_chunk
    key_pad = LANES
    assert n_new <= key_pad
    pad_keys = lambda a: jnp.pad(a.astype(BF16), ((0, 0), (0, key_pad - n_new), (0, 0)))
    per_b = lambda r, w: pl.BlockSpec((None, r, w), lambda b, pt: (b, 0, 0))
    return pl.pallas_call(
        functools.partial(_decode_kernel, layer=layer, n_chunks=n_chunks, pages_per_chunk=pages_per_chunk,
                          page=page, n_heads=n_heads, n_new=n_new, scale=scale),
        grid_spec=pltpu.PrefetchScalarGridSpec(
            num_scalar_prefetch=1,
            grid=(n_b,),
            in_specs=[
                per_b(n_rows, kv_lora), per_b(n_rows, rope), per_b(key_pad, kv_lora), per_b(key_pad, rope),
                pl.BlockSpec(memory_space=pl.ANY), pl.BlockSpec(memory_space=pl.ANY),
            ],
            out_specs=per_b(n_rows, kv_lora),
            scratch_shapes=[
                pltpu.VMEM((2, pages_per_chunk * page, kv_lora), F32),
                pltpu.VMEM((2, rope, pages_per_chunk * page), F32),
                pltpu.SemaphoreType.DMA((2, 2)),
            ],
        ),
        out_shape=jax.ShapeDtypeStruct((n_b, n_rows, kv_lora), F32),
        compiler_params=_params("arbitrary"),
    )(page_table, q_lat, q_rope, pad_keys(ckv_new), pad_keys(kr_new), cache_lat, cache_krt)


def _decode_out_kernel(ol_ref, x_ref, mod_ref, wuv_ref, wo_ref, o_ref, oh_ref, *, n_heads, kv_lora, v_dim):
    for i in range(n_heads):
        o_i = ol_ref[:, i * kv_lora:(i + 1) * kv_lora].astype(BF16)
        oh_ref[:, i * v_dim:(i + 1) * v_dim] = _dot(o_i, wuv_ref[i]).astype(BF16)
    o_ref[...] = x_ref[...] + mod_ref[5] * _dot(oh_ref[...], wo_ref[...])


def _decode_out(o_lat, x, mod4, w_uv, w_o, j):
    rows, d = x.shape
    kv_lora, n_heads, v_dim = w_uv.shape
    w_uv_h = jnp.transpose(w_uv, (1, 0, 2)).astype(BF16)
    tm = _tile(rows, 512)
    row = lambda w: pl.BlockSpec((tm, w), lambda i: (i, 0))
    return pl.pallas_call(
        functools.partial(_decode_out_kernel, n_heads=n_heads, kv_lora=kv_lora, v_dim=v_dim),
        grid=(rows // tm,),
        in_specs=[
            row(n_heads * kv_lora), row(d), _mod_spec(mod4, rows, tm),
            pl.BlockSpec(w_uv_h.shape, lambda i: (0, 0, 0)),
            pl.BlockSpec((None, n_heads * v_dim, d), lambda i: (j, 0, 0)),
        ],
        out_specs=row(d),
        out_shape=jax.ShapeDtypeStruct((rows, d), F32),
        scratch_shapes=[pltpu.VMEM((tm, n_heads * v_dim), BF16)],
        compiler_params=_params("arbitrary"),
    )(o_lat, x, mod4, w_uv_h, w_o)


def kernel(x_prompt, x_sample, c_prompt, c_sample, state_conv, cache_kv_latent, cache_k_rope, page_table, w_ada, b_ada, g_norm, w_ffn_gu, w_ffn_down, w_conv_in, w_conv_k, w_conv_out, w_mla_in, g_q_norm, g_kv_norm, w_uq, w_uk, w_uv, w_mla_out, g_final):
    n_p, seq_p, d = x_prompt.shape
    n_s, seq_s, _ = x_sample.shape
    depth = w_ada.shape[0]
    past = page_table.shape[1] * cache_kv_latent.shape[2]
    n_heads, qk_dim = w_uq.shape[2], w_uq.shape[3]
    nope, v_dim = w_uk.shape[3], w_uv.shape[3]
    kv_lora = g_kv_norm.shape[1]
    scale = qk_dim ** -0.5

    xp = x_prompt.reshape(n_p * seq_p, d)
    xs = x_sample.reshape(n_s * seq_s, d)
    mod = _adaln(jnp.concatenate([c_prompt, c_sample], axis=0), w_ada, b_ada)
    mod = mod.reshape(depth, n_p + n_s, N_MOD, d)
    g_norm4 = g_norm.reshape(depth, N_SUB, 1, d)
    w_gu = w_ffn_gu.astype(BF16)
    w_down = w_ffn_down.astype(BF16)
    w_cin = w_conv_in.astype(BF16)
    w_cout = w_conv_out.astype(BF16)
    w_mo = w_mla_out.astype(BF16)
    cache_krt = jnp.swapaxes(cache_k_rope, 2, 3)

    conv_p, conv_s, lat_p, kr_p, lat_s, kr_s = [], [], [], [], [], []
    for i in range(depth):
        mp = mod[i, :n_p].reshape(n_p, N_MOD, 1, d)
        ms = jnp.transpose(jnp.repeat(mod[i, n_p:], seq_s, axis=0), (1, 0, 2))[None]
        last = i == depth - 1
        xp = _ffn(xp, mp, g_norm4, w_gu, w_down, i, 0, 0)
        xs = _ffn(xs, ms, g_norm4, w_gu, w_down, i, 0, 0)
        j = i // N_MIXERS
        if i % N_MIXERS == 0:
            xp, st_p = _conv_prompt(xp, mp, g_norm4, w_cin, w_conv_k, w_cout, i, j, n_p)
            xs, st_s = _conv_sample(xs, ms, g_norm4, w_cin, w_conv_k, w_cout, state_conv[j], i, j)
            conv_p.append(st_p)
            conv_s.append(st_s)
        else:
            proj = (w_mla_in[j], g_q_norm[j], g_kv_norm[j], w_uq[j], w_uk[j], w_uv[j])
            qn, qr, kn, v, krp, ckv_p, k_p = _mla_project(
                xp, mp, g_norm4, i, *proj, jnp.arange(seq_p, dtype=jnp.int32), seq_p, absorb=False)
            xp = _attn_prompt(qn, qr, kn, v, krp, xp, mp, w_mo, j, n_p, n_heads, nope, v_dim, scale)
            pos_s = jnp.tile(past + jnp.arange(seq_s, dtype=jnp.int32), n_s)
            ql, qrs, ckv_s, k_s = _mla_project(xs, ms, g_norm4, i, *proj, pos_s, n_s * seq_s, absorb=True)
            o_lat = _decode_attn(
                page_table, ql.reshape(n_s, seq_s * n_heads, kv_lora), qrs.reshape(n_s, seq_s * n_heads, -1),
                ckv_s.reshape(n_s, seq_s, -1), k_s.reshape(n_s, seq_s, -1),
                cache_kv_latent, cache_krt, j, n_heads, scale)
            xs = _decode_out(o_lat.reshape(n_s * seq_s, n_heads * kv_lora), xs, ms, w_uv[j], w_mo, j)
            lat_p.append(ckv_p.reshape(n_p, seq_p, -1))
            kr_p.append(k_p.reshape(n_p, seq_p, -1))
            lat_s.append(ckv_s.reshape(n_s, seq_s, -1))
            kr_s.append(k_s.reshape(n_s, seq_s, -1))
        xp = _ffn(xp, mp, g_norm4, w_gu, w_down, i, 1, 2, g_final if last else None)
        xs = _ffn(xs, ms, g_norm4, w_gu, w_down, i, 1, 2, g_final if last else None)
    return (xp.reshape(n_p, seq_p, d), xs.reshape(n_s, seq_s, d), jnp.stack(conv_p), jnp.stack(conv_s),
            jnp.stack(lat_p), jnp.stack(kr_p), jnp.stack(lat_s), jnp.stack(kr_s))
```

```python
import functools

import jax
import jax.numpy as jnp
from jax import lax
from jax.experimental import pallas as pl
from jax.experimental.pallas import tpu as pltpu

EPS = 1e-6
ROPE_THETA = 10000.0
N_SUB = 3
N_MOD = 3 * N_SUB
N_MIXERS = 2

BF16 = jnp.bfloat16
F32 = jnp.float32

LANES = 128
VMEM_LIMIT_BYTES = 56 * 1024 * 1024
NEG_BIG = -1e30


def _params(*semantics):
    return pltpu.CompilerParams(dimension_semantics=semantics, vmem_limit_bytes=VMEM_LIMIT_BYTES)


def _dot(a, b):
    return jnp.dot(a, b, preferred_element_type=F32)


def _dot_nt(a, b):
    return lax.dot_general(a, b, (((1,), (1,)), ((), ())), preferred_element_type=F32)


def _rmsnorm(x, g):
    return x * lax.rsqrt(jnp.mean(x * x, axis=-1, keepdims=True) + EPS) * g


def _modulated(x, g, mod_ref, s):
    return _rmsnorm(x, g) * (1.0 + mod_ref[3 * s + 1]) + mod_ref[3 * s]


def _tile(n, want):
    t = min(n, want)
    while n % t:
        t -= 1
    return t


def _mod_spec(mod4, rows, tm):
    d = mod4.shape[-1]
    if mod4.shape[2] == 1:
        rows_per_seq = rows // mod4.shape[0]
        assert rows_per_seq % tm == 0
        tiles_per_seq = rows_per_seq // tm
        return pl.BlockSpec((None, N_MOD, 1, d), lambda i, *_: (i // tiles_per_seq, 0, 0, 0))
    assert mod4.shape[2] == rows
    return pl.BlockSpec((None, N_MOD, tm, d), lambda i, *_: (0, 0, i, 0))


def _adaln_kernel(c_ref, w_ref, b_ref, o_ref):
    c = c_ref[...]
    a = (c * jax.nn.sigmoid(c)).astype(BF16)
    o_ref[...] = _dot(a, w_ref[...].astype(BF16)) + b_ref[...]


def _adaln(c_all, w_ada, b_ada):
    depth, d, n_out = w_ada.shape
    n_seq = c_all.shape[0]
    tn = _tile(n_out, 1024)
    return pl.pallas_call(
        _adaln_kernel,
        grid=(depth, n_out // tn),
        in_specs=[
            pl.BlockSpec((n_seq, d), lambda l, j: (0, 0)),
            pl.BlockSpec((None, d, tn), lambda l, j: (l, 0, j)),
            pl.BlockSpec((None, 1, tn), lambda l, j: (l, 0, j)),
        ],
        out_specs=pl.BlockSpec((None, n_seq, tn), lambda l, j: (l, 0, j)),
        out_shape=jax.ShapeDtypeStruct((depth, n_seq, n_out), F32),
        compiler_params=_params("arbitrary", "arbitrary"),
    )(c_all, w_ada, b_ada.reshape(depth, 1, n_out))


def _ffn_kernel(*refs, s, tf, final_norm):
    if final_norm:
        x_ref, mod_ref, g_ref, wgu_ref, wd_ref, gf_ref, o_ref, h_ref, a_ref = refs
    else:
        x_ref, mod_ref, g_ref, wgu_ref, wd_ref, o_ref, h_ref, a_ref = refs
    f = wd_ref.shape[0]
    h_ref[...] = _modulated(x_ref[...], g_ref[...], mod_ref, s).astype(BF16)
    for c in range(f // tf):
        gate = _dot(h_ref[...], wgu_ref[:, c * tf:(c + 1) * tf])
        up = _dot(h_ref[...], wgu_ref[:, f + c * tf:f + (c + 1) * tf])
        a_ref[:, c * tf:(c + 1) * tf] = (gate * jax.nn.sigmoid(gate) * up).astype(BF16)
    xn = x_ref[...] + 0.5 * mod_ref[3 * s + 2] * _dot(a_ref[...], wd_ref[...])
    if final_norm:
        xn = _rmsnorm(xn, gf_ref[...])
    o_ref[...] = xn


def _resident(block_shape, index_map):
    return pl.BlockSpec(block_shape, index_map, pipeline_mode=pl.Buffered(1))


def _ffn(x, mod4, g_norm4, w_gu, w_down, layer, k, s, g_final=None):
    rows, d = x.shape
    f = w_down.shape[2]
    tm = _tile(rows if mod4.shape[2] != 1 else rows // mod4.shape[0], 512)
    tf = _tile(f, 256)
    in_specs = [
        pl.BlockSpec((tm, d), lambda i: (i, 0)),
        _mod_spec(mod4, rows, tm),
        _resident((None, None, 1, d), lambda i: (layer, s, 0, 0)),
        _resident((None, None, d, 2 * f), lambda i: (layer, k, 0, 0)),
        _resident((None, None, f, d), lambda i: (layer, k, 0, 0)),
    ]
    args = [x, mod4, g_norm4, w_gu, w_down]
    if g_final is not None:
        in_specs.append(_resident((1, d), lambda i: (0, 0)))
        args.append(g_final.reshape(1, d))
    return pl.pallas_call(
        functools.partial(_ffn_kernel, s=s, tf=tf, final_norm=g_final is not None),
        grid=(rows // tm,),
        in_specs=in_specs,
        out_specs=pl.BlockSpec((tm, d), lambda i: (i, 0)),
        out_shape=jax.ShapeDtypeStruct((rows, d), F32),
        scratch_shapes=[pltpu.VMEM((tm, d), BF16), pltpu.VMEM((tm, f), BF16)],
        compiler_params=_params("arbitrary"),
    )(*args)


def _conv_taps(u, u1, u2, wk_ref):
    return wk_ref[0:1, :] * u2 + wk_ref[1:2, :] * u1 + wk_ref[2:3, :] * u


def _conv_gates(x_ref, mod_ref, g_ref, win_ref):
    d = x_ref.shape[-1]
    h = _modulated(x_ref[...], g_ref[...], mod_ref, 1).astype(BF16)
    p = _dot(h, win_ref[...])
    return p[:, :d], p[:, d:2 * d] * p[:, 2 * d:]


def _conv_finish(x_ref, mod_ref, b_gate, conv, wout_ref, o_ref):
    y = _dot((b_gate * conv).astype(BF16), wout_ref[...])
    o_ref[...] = x_ref[...] + mod_ref[5] * y


def _conv_seq_kernel(x_ref, mod_ref, g_ref, win_ref, wk_ref, wout_ref, o_ref, st_ref, carry_ref, *, tiles_per_seq):
    tm = x_ref.shape[0]

    @pl.when(pl.program_id(0) % tiles_per_seq == 0)
    def _():
        carry_ref[...] = jnp.zeros_like(carry_ref)

    b_gate, u = _conv_gates(x_ref, mod_ref, g_ref, win_ref)
    row = lax.broadcasted_iota(jnp.int32, (tm, 1), 0)
    prev0 = carry_ref[0:1, :]
    prev1 = carry_ref[1:2, :]
    u1 = jnp.where(row == 0, prev1, pltpu.roll(u, 1, axis=0))
    u2 = jnp.where(row == 0, prev0, jnp.where(row == 1, prev1, pltpu.roll(u, 2, axis=0)))
    _conv_finish(x_ref, mod_ref, b_gate, _conv_taps(u, u1, u2, wk_ref), wout_ref, o_ref)
    carry_ref[...] = u[tm - 2:, :]
    st_ref[...] = u[tm - 2:, :]


def _conv_step_kernel(x_ref, mod_ref, g_ref, win_ref, wk_ref, wout_ref, p1_ref, p2_ref, o_ref, u_ref, *, seq):
    tm = x_ref.shape[0]
    b_gate, u = _conv_gates(x_ref, mod_ref, g_ref, win_ref)
    t = lax.broadcasted_iota(jnp.int32, (tm, 1), 0) % seq
    u1 = jnp.where(t >= 1, pltpu.roll(u, 1, axis=0), p1_ref[...])
    u2 = jnp.where(t >= 2, pltpu.roll(u, 2, axis=0), p2_ref[...])
    _conv_finish(x_ref, mod_ref, b_gate, _conv_taps(u, u1, u2, wk_ref), wout_ref, o_ref)
    u_ref[...] = u


def _conv_weight_specs(d, n_taps, layer, j):
    return [
        _resident((None, None, 1, d), lambda i: (layer, 1, 0, 0)),
        _resident((None, d, 3 * d), lambda i: (j, 0, 0)),
        _resident((None, n_taps, d), lambda i: (j, 0, 0)),
        _resident((None, d, d), lambda i: (j, 0, 0)),
    ]


def _conv_prompt(x, mod4, g_norm4, w_in, w_k, w_out, layer, j, n_seq):
    rows, d = x.shape
    seq = rows // n_seq
    assert w_k.shape[1] == 3 and seq >= 2
    tm = _tile(seq, 512)
    assert tm >= 2
    tiles_per_seq = seq // tm
    return pl.pallas_call(
        functools.partial(_conv_seq_kernel, tiles_per_seq=tiles_per_seq),
        grid=(rows // tm,),
        in_specs=[pl.BlockSpec((tm, d), lambda i: (i, 0)), _mod_spec(mod4, rows, tm)]
        + _conv_weight_specs(d, 3, layer, j),
        out_specs=[
            pl.BlockSpec((tm, d), lambda i: (i, 0)),
            pl.BlockSpec((None, 2, d), lambda i: (i // tiles_per_seq, 0, 0)),
        ],
        out_shape=[jax.ShapeDtypeStruct((rows, d), F32), jax.ShapeDtypeStruct((n_seq, 2, d), F32)],
        scratch_shapes=[pltpu.VMEM((2, d), F32)],
        compiler_params=_params("arbitrary"),
    )(x, mod4, g_norm4, w_in, w_k, w_out)


def _conv_sample(x, mod4, g_norm4, w_in, w_k, w_out, state, layer, j):
    rows, d = x.shape
    n_seq = state.shape[0]
    seq = rows // n_seq
    assert w_k.shape[1] == 3 and seq >= 2
    zeros = lambda n: jnp.zeros((n_seq, n, d), F32)
    prev1 = jnp.concatenate([state[:, 1:2], zeros(seq - 1)], axis=1).reshape(rows, d)
    prev2 = jnp.concatenate([state, zeros(seq - 2)], axis=1).reshape(rows, d)
    seqs_per_tile = _tile(n_seq, max(1, 512 // seq))
    tm = seqs_per_tile * seq
    row_spec = pl.BlockSpec((tm, d), lambda i: (i, 0))
    x_new, u = pl.pallas_call(
        functools.partial(_conv_step_kernel, seq=seq),
        grid=(rows // tm,),
        in_specs=[row_spec, _mod_spec(mod4, rows, tm)] + _conv_weight_specs(d, 3, layer, j) + [row_spec, row_spec],
        out_specs=[row_spec, row_spec],
        out_shape=[jax.ShapeDtypeStruct((rows, d), F32)] * 2,
        compiler_params=_params("arbitrary"),
    )(x, mod4, g_norm4, w_in, w_k, w_out, prev1, prev2)
    return x_new, u.reshape(n_seq, seq, d)[:, seq - 2:]


def _mla_proj_kernel(*refs, dims, absorb):
    q_lora, kv_lora, rope, n_heads, nope, rw = dims
    if absorb:
        (x_ref, mod_ref, g_ref, win_ref, gq_ref, gkv_ref, wuq_ref, ck_ref, sk_ref, cq_ref, sq_ref, wukt_ref,
         ql_ref, qr_ref, ckv_ref, kr_ref) = refs
    else:
        (x_ref, mod_ref, g_ref, win_ref, gq_ref, gkv_ref, wuq_ref, ck_ref, sk_ref, wuk_ref, wuv_ref,
         q_ref, k_ref, v_ref, ckv_ref, kr_ref) = refs
    h = _modulated(x_ref[...], g_ref[...], mod_ref, 1).astype(BF16)
    dn = _dot(h, win_ref[...])
    cq = _rmsnorm(dn[:, :q_lora], gq_ref[...]).astype(BF16)
    ckv = _rmsnorm(dn[:, q_lora:q_lora + kv_lora], gkv_ref[...])
    ckv_ref[...] = ckv
    o = q_lora + kv_lora
    krp = dn[:, o:o + LANES] * ck_ref[...] + dn[:, o + LANES:o + 2 * LANES] * sk_ref[...]
    kr_ref[...] = krp[:, :rope]
    q = _dot(cq, wuq_ref[...])
    hn = n_heads * nope
    hr = n_heads * rw
    if absorb:
        qr_ref[...] = (q[:, hn:hn + hr] * cq_ref[...] + q[:, hn + hr:hn + 2 * hr] * sq_ref[...]).astype(BF16)
        for i in range(n_heads):
            qn_i = q[:, i * nope:(i + 1) * nope].astype(BF16)
            ql_ref[:, i * kv_lora:(i + 1) * kv_lora] = _dot(qn_i, wukt_ref[i]).astype(BF16)
    else:
        ckv_b = ckv.astype(BF16)
        kn = _dot(ckv_b, wuk_ref[...])
        krp_b = krp.astype(BF16)
        w = nope + rw
        for i in range(n_heads):
            lo, hi = hn + i * rw, hn + (i + 1) * rw
            q_ref[:, i * w:i * w + nope] = q[:, i * nope:(i + 1) * nope].astype(BF16)
            q_ref[:, i * w + nope:(i + 1) * w] = (q[:, lo:hi] * ck_ref[...] + q[:, hr + lo:hr + hi] * sk_ref[...]).astype(BF16)
            k_ref[:, i * w:i * w + nope] = kn[:, i * nope:(i + 1) * nope].astype(BF16)
            k_ref[:, i * w + nope:(i + 1) * w] = krp_b
        v_ref[...] = _dot(ckv_b, wuv_ref[...]).astype(BF16)


def _rope_tables(pos, rope, width, reps):
    half = rope // 2
    inv = ROPE_THETA ** (-jnp.arange(half, dtype=F32) * (2.0 / rope))
    ang = pos.astype(F32)[:, None] * inv[None, :]
    cos, sin = jnp.cos(ang), jnp.sin(ang)
    pad = jnp.zeros((pos.shape[0], width - rope), F32)
    c = jnp.concatenate([cos, cos, pad], axis=1)
    s = jnp.concatenate([-sin, sin, pad], axis=1)
    return jnp.tile(c, (1, reps)), jnp.tile(s, (1, reps))


def _swap_halves(w, rope):
    half = rope // 2
    return jnp.concatenate([w[..., half:], w[..., :half]], axis=-1)


def _pad_last(w, width):
    return jnp.pad(w, [(0, 0)] * (w.ndim - 1) + [(0, width - w.shape[-1])])


def _mla_weights(w_in, w_uq, q_lora, kv_lora, rope, nope, rw):
    d = w_in.shape[0]
    n_heads = w_uq.shape[1]
    w_kr = w_in[:, q_lora + kv_lora:]
    w_in_ext = jnp.concatenate(
        [w_in[:, :q_lora + kv_lora], _pad_last(w_kr, LANES), _pad_last(_swap_halves(w_kr, rope), LANES)], axis=1)
    w_qr = w_uq[:, :, nope:]
    w_uq_ext = jnp.concatenate([
        w_uq[:, :, :nope].reshape(q_lora, n_heads * nope),
        _pad_last(w_qr, rw).reshape(q_lora, n_heads * rw),
        _pad_last(_swap_halves(w_qr, rope), rw).reshape(q_lora, n_heads * rw)], axis=1)
    assert w_in_ext.shape == (d, q_lora + kv_lora + 2 * LANES)
    return w_in_ext.astype(BF16), w_uq_ext.astype(BF16)


def _mla_project(x, mod4, g_norm4, layer, w_in, g_q, g_kv, w_uq, w_uk, w_uv, pos_rows, rows_per_pos_table, absorb):
    rows, d = x.shape
    q_lora, kv_lora = g_q.shape[0], g_kv.shape[0]
    rope = w_in.shape[1] - q_lora - kv_lora
    n_heads, nope = w_uk.shape[1], w_uk.shape[2]
    v_dim = w_uv.shape[2]
    rw = rope if absorb else LANES
    w_in_ext, w_uq_ext = _mla_weights(w_in, w_uq, q_lora, kv_lora, rope, nope, rw)
    cos_k, sin_k = _rope_tables(pos_rows, rope, LANES, 1)
    period = rows_per_pos_table
    tm = _tile(period, 512)
    tiles_per_period = period // tm
    row = lambda w: pl.BlockSpec((tm, w), lambda i: (i, 0))
    tab = lambda w: pl.BlockSpec((tm, w), lambda i: (i % tiles_per_period, 0))
    full = lambda a: _resident(a.shape, lambda i: (0,) * a.ndim)
    hn, hr = n_heads * nope, n_heads * rw
    in_specs = [
        row(d), _mod_spec(mod4, rows, tm),
        _resident((None, None, 1, d), lambda i: (layer, 1, 0, 0)),
        full(w_in_ext), _resident((1, q_lora), lambda i: (0, 0)), _resident((1, kv_lora), lambda i: (0, 0)),
        full(w_uq_ext), tab(LANES), tab(LANES),
    ]
    args = [x, mod4, g_norm4, w_in_ext, g_q.reshape(1, q_lora), g_kv.reshape(1, kv_lora), w_uq_ext, cos_k, sin_k]
    if absorb:
        cos_q, sin_q = _rope_tables(pos_rows, rope, rw, n_heads)
        w_ukt = jnp.transpose(w_uk, (1, 2, 0)).astype(BF16)
        in_specs += [tab(hr), tab(hr), full(w_ukt)]
        args += [cos_q, sin_q, w_ukt]
        out_widths = [(n_heads * kv_lora, BF16), (hr, BF16), (kv_lora, F32), (rope, F32)]
    else:
        w_uk2 = w_uk.reshape(kv_lora, hn).astype(BF16)
        w_uv2 = w_uv.reshape(kv_lora, n_heads * v_dim).astype(BF16)
        in_specs += [full(w_uk2), full(w_uv2)]
        args += [w_uk2, w_uv2]
        out_widths = [(hn + hr, BF16), (hn + hr, BF16), (n_heads * v_dim, BF16), (kv_lora, F32), (rope, F32)]
    return pl.pallas_call(
        functools.partial(_mla_proj_kernel, dims=(q_lora, kv_lora, rope, n_heads, nope, rw), absorb=absorb),
        grid=(rows // tm,),
        in_specs=in_specs,
        out_specs=[row(w) for w, _ in out_widths],
        out_shape=[jax.ShapeDtypeStruct((rows, w), dt) for w, dt in out_widths],
        compiler_params=_params("arbitrary"),
    )(*args)


def _attn_kernel(q_ref, k_ref, v_ref, x_ref, mod_ref, wo_ref, o_ref, oh_ref,
                 *, n_heads, heads_per_step, v_dim, tk, scale):
    tq = q_ref.shape[0]
    w = q_ref.shape[1] // n_heads
    q0 = pl.program_id(1) * tq
    n_full = q0 // tk
    n_kv = (q0 + tq + tk - 1) // tk
    row = q0 + lax.broadcasted_iota(jnp.int32, (tq, 1), 0)
    col0 = lax.broadcasted_iota(jnp.int32, (1, tk), 1)

    def step(c, carry, heads, masked):
        off = pl.multiple_of(c * tk, tk)
        keys = pl.ds(off, tk)
        out = []
        for i, (m, l, acc) in zip(heads, carry):
            s = _dot_nt(q_ref[:, i * w:(i + 1) * w], k_ref[keys, i * w:(i + 1) * w]) * scale
            if masked:
                s = jnp.where(row >= off + col0, s, -jnp.inf)
            m_new = jnp.maximum(m, jnp.max(s, axis=-1, keepdims=True))
            alpha = jnp.exp(m - m_new)
            p = jnp.exp(s - m_new)
            l = alpha * l + jnp.sum(p, axis=-1, keepdims=True)
            acc = alpha * acc + _dot(p.astype(BF16), v_ref[keys, i * v_dim:(i + 1) * v_dim])
            out.append((m_new, l, acc))
        return tuple(out)

    init = (jnp.full((tq, 1), NEG_BIG, F32), jnp.zeros((tq, 1), F32), jnp.zeros((tq, v_dim), F32))
    for g in range(0, n_heads, heads_per_step):
        heads = tuple(range(g, g + heads_per_step))
        carry = lax.fori_loop(0, n_full, functools.partial(step, heads=heads, masked=False), (init,) * heads_per_step)
        carry = lax.fori_loop(n_full, n_kv, functools.partial(step, heads=heads, masked=True), carry)
        for i, (_, l, acc) in zip(heads, carry):
            oh_ref[:, i * v_dim:(i + 1) * v_dim] = (acc / l).astype(BF16)
    o_ref[...] = x_ref[...] + mod_ref[5] * _dot(oh_ref[...], wo_ref[...])


def _attn_prompt(q, k, v, x, mod4, w_o, j, n_seq, n_heads, v_dim, scale):
    rows, d = x.shape
    seq = rows // n_seq
    tq = _tile(seq, 512)
    tk = tq
    r3 = lambda a: a.reshape(n_seq, seq, a.shape[-1])
    q_spec = lambda w: pl.BlockSpec((None, tq, w), lambda b, i: (b, i, 0))
    kv_spec = lambda w: pl.BlockSpec((None, seq, w), lambda b, i: (b, 0, 0))
    out = pl.pallas_call(
        functools.partial(_attn_kernel, n_heads=n_heads, heads_per_step=2 if n_heads % 2 == 0 else 1,
                          v_dim=v_dim, tk=tk, scale=scale),
        grid=(n_seq, seq // tq),
        in_specs=[
            q_spec(q.shape[-1]), kv_spec(k.shape[-1]), kv_spec(v.shape[-1]), q_spec(d),
            pl.BlockSpec((None, N_MOD, 1, d), lambda b, i: (b, 0, 0, 0)),
            _resident((None, n_heads * v_dim, d), lambda b, i: (j, 0, 0)),
        ],
        out_specs=q_spec(d),
        out_shape=jax.ShapeDtypeStruct((n_seq, seq, d), F32),
        scratch_shapes=[pltpu.VMEM((tq, n_heads * v_dim), BF16)],
        compiler_params=_params("arbitrary", "arbitrary"),
    )(r3(q), r3(k), r3(v), r3(x), mod4, w_o)
    return out.reshape(rows, d)


def _decode_kernel(pt_ref, ql_ref, qr_ref, cn_ref, kn_ref, lat_hbm, krt_hbm, o_ref, lat_buf, krt_buf, sem,
                   *, layer, n_chunks, pages_per_chunk, page, n_heads, n_new, n_split, scale):
    b = pl.program_id(0)
    n_b = pl.num_programs(0)
    n_rows = ql_ref.shape[0]

    def copies(bb, c, slot):
        out = []
        for k in range(pages_per_chunk):
            pg = pt_ref[bb, c * pages_per_chunk + k]
            dst = pl.ds(k * page, page)
            out.append(pltpu.make_async_copy(lat_hbm.at[layer, pg], lat_buf.at[slot, dst], sem.at[0, slot]))
            out.append(pltpu.make_async_copy(krt_hbm.at[layer, pg], krt_buf.at[slot, :, dst], sem.at[1, slot]))
        return out

    def start(bb, c, slot):
        for cp in copies(bb, c, slot):
            cp.start()

    @pl.when(b == 0)
    def _():
        start(0, 0, 0)

    ql = ql_ref[...]
    qr = qr_ref[...]

    def online(carry, s, values):
        m, l, acc = carry
        m_new = jnp.maximum(m, jnp.max(s, axis=-1, keepdims=True))
        alpha = jnp.exp(m - m_new)
        p = jnp.exp(s - m_new)
        l = alpha * l + jnp.sum(p, axis=-1, keepdims=True)
        acc = alpha * acc + _dot(p.astype(BF16), values)
        return m_new, l, acc

    def step(c, carry):
        slot = (b * n_chunks + c) % 2
        last = c + 1 == n_chunks
        nb = jnp.where(last, b + 1, b)
        nc = jnp.where(last, 0, c + 1)

        @pl.when(nb < n_b)
        def _():
            start(nb, nc, 1 - slot)

        for cp in copies(b, c, slot):
            cp.wait()
        new = []
        for k, state in enumerate(carry):
            keys = slice(k * sub, (k + 1) * sub)
            lat = lat_buf[slot, keys, :].astype(BF16)
            krt = krt_buf[slot, :, keys].astype(BF16)
            s = (_dot_nt(ql, lat) + _dot(qr, krt)) * scale
            new.append(online(state, s, lat))
        return tuple(new)

    sub = pages_per_chunk * page // n_split
    init = (jnp.full((n_rows, 1), NEG_BIG, F32), jnp.zeros((n_rows, 1), F32),
            jnp.zeros((n_rows, lat_buf.shape[-1]), F32))
    states = lax.fori_loop(0, n_chunks, step, (init,) * n_split)
    m = functools.reduce(jnp.maximum, [st[0] for st in states])
    weights = [jnp.exp(st[0] - m) for st in states]
    carry = (m, sum(w * st[1] for w, st in zip(weights, states)), sum(w * st[2] for w, st in zip(weights, states)))

    cn = cn_ref[...]
    s_new = (_dot_nt(ql, cn) + _dot_nt(qr, kn_ref[...])) * scale
    t = lax.broadcasted_iota(jnp.int32, (n_rows, 1), 0) // n_heads
    u = lax.broadcasted_iota(jnp.int32, (1, cn.shape[0]), 1)
    s_new = jnp.where((u <= t) & (u < n_new), s_new, -jnp.inf)
    _, l, acc = online(carry, s_new, cn)
    o_ref[...] = acc / l


def _decode_attn(page_table, q_lat, q_rope, ckv_new, kr_new, cache_lat, cache_krt, layer, n_heads, scale):
    n_b, n_rows, kv_lora = q_lat.shape
    rope = q_rope.shape[-1]
    n_new = n_rows // n_heads
    page = cache_lat.shape[2]
    n_pages = page_table.shape[1]
    pages_per_chunk = _tile(n_pages, 16)
    n_chunks = n_pages // pages_per_chunk
    key_pad = LANES
    assert n_new <= key_pad
    pad_keys = lambda a: jnp.pad(a.astype(BF16), ((0, 0), (0, key_pad - n_new), (0, 0)))
    per_b = lambda r, w: pl.BlockSpec((None, r, w), lambda b, pt: (b, 0, 0))
    return pl.pallas_call(
        functools.partial(_decode_kernel, layer=layer, n_chunks=n_chunks, pages_per_chunk=pages_per_chunk,
                          page=page, n_heads=n_heads, n_new=n_new, n_split=2 if pages_per_chunk % 2 == 0 else 1,
                          scale=scale),
        grid_spec=pltpu.PrefetchScalarGridSpec(
            num_scalar_prefetch=1,
            grid=(n_b,),
            in_specs=[
                per_b(n_rows, kv_lora), per_b(n_rows, rope), per_b(key_pad, kv_lora), per_b(key_pad, rope),
                pl.BlockSpec(memory_space=pl.ANY), pl.BlockSpec(memory_space=pl.ANY),
            ],
            out_specs=per_b(n_rows, kv_lora),
            scratch_shapes=[
                pltpu.VMEM((2, pages_per_chunk * page, kv_lora), F32),
                pltpu.VMEM((2, rope, pages_per_chunk * page), F32),
                pltpu.SemaphoreType.DMA((2, 2)),
            ],
        ),
        out_shape=jax.ShapeDtypeStruct((n_b, n_rows, kv_lora), F32),
        compiler_params=_params("arbitrary"),
    )(page_table, q_lat, q_rope, pad_keys(ckv_new), pad_keys(kr_new), cache_lat, cache_krt)


def _decode_out_kernel(ol_ref, x_ref, mod_ref, wuv_ref, wo_ref, o_ref, oh_ref, *, n_heads, kv_lora, v_dim):
    for i in range(n_heads):
        o_i = ol_ref[:, i * kv_lora:(i + 1) * kv_lora].astype(BF16)
        oh_ref[:, i * v_dim:(i + 1) * v_dim] = _dot(o_i, wuv_ref[i]).astype(BF16)
    o_ref[...] = x_ref[...] + mod_ref[5] * _dot(oh_ref[...], wo_ref[...])


def _decode_out(o_lat, x, mod4, w_uv, w_o, j):
    rows, d = x.shape
    kv_lora, n_heads, v_dim = w_uv.shape
    w_uv_h = jnp.transpose(w_uv, (1, 0, 2)).astype(BF16)
    tm = _tile(rows, 512)
    row = lambda w: pl.BlockSpec((tm, w), lambda i: (i, 0))
    return pl.pallas_call(
        functools.partial(_decode_out_kernel, n_heads=n_heads, kv_lora=kv_lora, v_dim=v_dim),
        grid=(rows // tm,),
        in_specs=[
            row(n_heads * kv_lora), row(d), _mod_spec(mod4, rows, tm),
            _resident(w_uv_h.shape, lambda i: (0, 0, 0)),
            _resident((None, n_heads * v_dim, d), lambda i: (j, 0, 0)),
        ],
        out_specs=row(d),
        out_shape=jax.ShapeDtypeStruct((rows, d), F32),
        scratch_shapes=[pltpu.VMEM((tm, n_heads * v_dim), BF16)],
        compiler_params=_params("arbitrary"),
    )(o_lat, x, mod4, w_uv_h, w_o)


def kernel(x_prompt, x_sample, c_prompt, c_sample, state_conv, cache_kv_latent, cache_k_rope, page_table, w_ada, b_ada, g_norm, w_ffn_gu, w_ffn_down, w_conv_in, w_conv_k, w_conv_out, w_mla_in, g_q_norm, g_kv_norm, w_uq, w_uk, w_uv, w_mla_out, g_final):
    n_p, seq_p, d = x_prompt.shape
    n_s, seq_s, _ = x_sample.shape
    depth = w_ada.shape[0]
    past = page_table.shape[1] * cache_kv_latent.shape[2]
    n_heads, qk_dim = w_uq.shape[2], w_uq.shape[3]
    v_dim = w_uv.shape[3]
    kv_lora = g_kv_norm.shape[1]
    scale = qk_dim ** -0.5

    xp = x_prompt.reshape(n_p * seq_p, d)
    xs = x_sample.reshape(n_s * seq_s, d)
    mod = _adaln(jnp.concatenate([c_prompt, c_sample], axis=0), w_ada, b_ada)
    mod = mod.reshape(depth, n_p + n_s, N_MOD, d)
    g_norm4 = g_norm.reshape(depth, N_SUB, 1, d)
    w_gu = w_ffn_gu.astype(BF16)
    w_down = w_ffn_down.astype(BF16)
    w_cin = w_conv_in.astype(BF16)
    w_cout = w_conv_out.astype(BF16)
    w_mo = w_mla_out.astype(BF16)
    cache_krt = jnp.swapaxes(cache_k_rope, 2, 3)

    conv_p, conv_s, lat_p, kr_p, lat_s, kr_s = [], [], [], [], [], []
    for i in range(depth):
        mp = mod[i, :n_p].reshape(n_p, N_MOD, 1, d)
        ms = jnp.transpose(jnp.repeat(mod[i, n_p:], seq_s, axis=0), (1, 0, 2))[None]
        last = i == depth - 1
        xp = _ffn(xp, mp, g_norm4, w_gu, w_down, i, 0, 0)
        xs = _ffn(xs, ms, g_norm4, w_gu, w_down, i, 0, 0)
        j = i // N_MIXERS
        if i % N_MIXERS == 0:
            xp, st_p = _conv_prompt(xp, mp, g_norm4, w_cin, w_conv_k, w_cout, i, j, n_p)
            xs, st_s = _conv_sample(xs, ms, g_norm4, w_cin, w_conv_k, w_cout, state_conv[j], i, j)
            conv_p.append(st_p)
            conv_s.append(st_s)
        else:
            proj = (w_mla_in[j], g_q_norm[j], g_kv_norm[j], w_uq[j], w_uk[j], w_uv[j])
            q, k, v, ckv_p, k_p = _mla_project(
                xp, mp, g_norm4, i, *proj, jnp.arange(seq_p, dtype=jnp.int32), seq_p, absorb=False)
            xp = _attn_prompt(q, k, v, xp, mp, w_mo, j, n_p, n_heads, v_dim, scale)
            pos_s = jnp.tile(past + jnp.arange(seq_s, dtype=jnp.int32), n_s)
            ql, qrs, ckv_s, k_s = _mla_project(xs, ms, g_norm4, i, *proj, pos_s, n_s * seq_s, absorb=True)
            o_lat = _decode_attn(
                page_table, ql.reshape(n_s, seq_s * n_heads, kv_lora), qrs.reshape(n_s, seq_s * n_heads, -1),
                ckv_s.reshape(n_s, seq_s, -1), k_s.reshape(n_s, seq_s, -1),
                cache_kv_latent, cache_krt, j, n_heads, scale)
            xs = _decode_out(o_lat.reshape(n_s * seq_s, n_heads * kv_lora), xs, ms, w_uv[j], w_mo, j)
            lat_p.append(ckv_p.reshape(n_p, seq_p, -1))
            kr_p.append(k_p.reshape(n_p, seq_p, -1))
            lat_s.append(ckv_s.reshape(n_s, seq_s, -1))
            kr_s.append(k_s.reshape(n_s, seq_s, -1))
        xp = _ffn(xp, mp, g_norm4, w_gu, w_down, i, 1, 2, g_final if last else None)
        xs = _ffn(xs, ms, g_norm4, w_gu, w_down, i, 1, 2, g_final if last else None)
    return (xp.reshape(n_p, seq_p, d), xs.reshape(n_s, seq_s, d), jnp.stack(conv_p), jnp.stack(conv_s),
            jnp.stack(lat_p), jnp.stack(kr_p), jnp.stack(lat_s), jnp.stack(kr_s))
```

```python
import functools

import jax
import jax.numpy as jnp
from jax import lax
from jax.experimental import pallas as pl
from jax.experimental.pallas import tpu as pltpu

EPS = 1e-6
ROPE_THETA = 10000.0
N_SUB = 3
N_MOD = 3 * N_SUB
N_MIXERS = 2
MIXER_GATE = 3 * 1 + 2

BF16 = jnp.bfloat16
F32 = jnp.float32

LANES = 128
VMEM_LIMIT_BYTES = 56 * 1024 * 1024
NEG_BIG = -1e30
LOG2_E = 1.4426950408889634


def _params(*semantics):
    return pltpu.CompilerParams(dimension_semantics=semantics, vmem_limit_bytes=VMEM_LIMIT_BYTES)


def _dot(a, b):
    return jnp.dot(a, b, preferred_element_type=F32)


def _dot_nt(a, b):
    return lax.dot_general(a, b, (((1,), (1,)), ((), ())), preferred_element_type=F32)


def _rmsnorm(x, g):
    return x * lax.rsqrt(jnp.mean(x * x, axis=-1, keepdims=True) + EPS) * g


def _mod(mod_ref, k):
    d = mod_ref.shape[-1] // N_MOD
    return mod_ref[:, k * d:(k + 1) * d]


def _modulated(x, g, mod_ref, s):
    return _rmsnorm(x, g) * (1.0 + _mod(mod_ref, 3 * s + 1)) + _mod(mod_ref, 3 * s)


def _tile(n, want):
    t = min(n, want)
    while n % t:
        t -= 1
    return t


def _resident(block_shape, index_map):
    return pl.BlockSpec(block_shape, index_map, pipeline_mode=pl.Buffered(1))


def _rows_per_mod(mod, rows):
    return rows // mod.shape[1] if mod.ndim == 4 else 1


def _mod_spec(mod, layer, rows, tm):
    width = mod.shape[-1]
    if mod.ndim == 4:
        rows_per_seq = _rows_per_mod(mod, rows)
        assert rows_per_seq % tm == 0
        tiles_per_seq = rows_per_seq // tm
        return pl.BlockSpec((None, None, 1, width), lambda i, *_: (layer, i // tiles_per_seq, 0, 0))
    assert mod.shape[1] == rows
    return pl.BlockSpec((None, tm, width), lambda i, *_: (layer, i, 0))


def _adaln_kernel(cp_ref, cs_ref, w_ref, b_ref, op_ref, os_ref):
    w = w_ref[...].astype(BF16)
    for c_ref, o_ref in ((cp_ref, op_ref), (cs_ref, os_ref)):
        c = c_ref[...]
        o_ref[...] = _dot((c * jax.nn.sigmoid(c)).astype(BF16), w) + b_ref[...]


def _adaln(c_p, c_s, w_ada, b_ada):
    depth, d, n_out = w_ada.shape
    tn = _tile(n_out, 1024)
    c_spec = lambda c: _resident(c.shape, lambda l, j: (0, 0))
    o_spec = lambda c: pl.BlockSpec((None, c.shape[0], tn), lambda l, j: (l, 0, j))
    return pl.pallas_call(
        _adaln_kernel,
        grid=(depth, n_out // tn),
        in_specs=[
            c_spec(c_p), c_spec(c_s),
            pl.BlockSpec((None, d, tn), lambda l, j: (l, 0, j)),
            pl.BlockSpec((None, 1, tn), lambda l, j: (l, 0, j)),
        ],
        out_specs=[o_spec(c_p), o_spec(c_s)],
        out_shape=[jax.ShapeDtypeStruct((depth, c.shape[0], n_out), F32) for c in (c_p, c_s)],
        compiler_params=_params("arbitrary", "arbitrary"),
    )(c_p, c_s, w_ada, b_ada.reshape(depth, 1, n_out))


def _ffn_kernel(*refs, s, tf, final_norm):
    if final_norm:
        x_ref, mod_ref, g_ref, wgu_ref, wd_ref, gf_ref, o_ref, h_ref, a_ref = refs
    else:
        x_ref, mod_ref, g_ref, wgu_ref, wd_ref, o_ref, h_ref, a_ref = refs
    f = wd_ref.shape[0]
    h_ref[...] = _modulated(x_ref[...], g_ref[...], mod_ref, s).astype(BF16)
    for c in range(f // tf):
        gate = _dot(h_ref[...], wgu_ref[:, c * tf:(c + 1) * tf])
        up = _dot(h_ref[...], wgu_ref[:, f + c * tf:f + (c + 1) * tf])
        a_ref[:, c * tf:(c + 1) * tf] = (gate * jax.nn.sigmoid(gate) * up).astype(BF16)
    xn = x_ref[...] + 0.5 * _mod(mod_ref, 3 * s + 2) * _dot(a_ref[...], wd_ref[...])
    if final_norm:
        xn = _rmsnorm(xn, gf_ref[...])
    o_ref[...] = xn


def _ffn(x, mod, g_norm4, w_gu, w_down, layer, k, s, g_final=None):
    rows, d = x.shape
    f = w_down.shape[2]
    tm = _tile(rows if mod.ndim == 3 else _rows_per_mod(mod, rows), 512)
    tf = _tile(f, 256)
    in_specs = [
        pl.BlockSpec((tm, d), lambda i: (i, 0)),
        _mod_spec(mod, layer, rows, tm),
        _resident((None, None, 1, d), lambda i: (layer, s, 0, 0)),
        _resident((None, None, d, 2 * f), lambda i: (layer, k, 0, 0)),
        _resident((None, None, f, d), lambda i: (layer, k, 0, 0)),
    ]
    args = [x, mod, g_norm4, w_gu, w_down]
    if g_final is not None:
        in_specs.append(_resident((1, d), lambda i: (0, 0)))
        args.append(g_final.reshape(1, d))
    return pl.pallas_call(
        functools.partial(_ffn_kernel, s=s, tf=tf, final_norm=g_final is not None),
        grid=(rows // tm,),
        in_specs=in_specs,
        out_specs=pl.BlockSpec((tm, d), lambda i: (i, 0)),
        out_shape=jax.ShapeDtypeStruct((rows, d), F32),
        scratch_shapes=[pltpu.VMEM((tm, d), BF16), pltpu.VMEM((tm, f), BF16)],
        compiler_params=_params("arbitrary"),
    )(*args)


def _conv_taps(u, u1, u2, wk_ref):
    return wk_ref[0:1, :] * u2 + wk_ref[1:2, :] * u1 + wk_ref[2:3, :] * u


def _conv_gates(x_ref, mod_ref, g_ref, win_ref):
    d = x_ref.shape[-1]
    h = _modulated(x_ref[...], g_ref[...], mod_ref, 1).astype(BF16)
    p = _dot(h, win_ref[...])
    return p[:, :d], p[:, d:2 * d] * p[:, 2 * d:]


def _conv_finish(x_ref, mod_ref, b_gate, conv, wout_ref, o_ref):
    y = _dot((b_gate * conv).astype(BF16), wout_ref[...])
    o_ref[...] = x_ref[...] + _mod(mod_ref, MIXER_GATE) * y


def _conv_seq_kernel(x_ref, mod_ref, g_ref, win_ref, wk_ref, wout_ref, o_ref, st_ref, carry_ref, *, tiles_per_seq):
    tm = x_ref.shape[0]

    @pl.when(pl.program_id(0) % tiles_per_seq == 0)
    def _():
        carry_ref[...] = jnp.zeros_like(carry_ref)

    b_gate, u = _conv_gates(x_ref, mod_ref, g_ref, win_ref)
    row = lax.broadcasted_iota(jnp.int32, (tm, 1), 0)
    prev0 = carry_ref[0:1, :]
    prev1 = carry_ref[1:2, :]
    u1 = jnp.where(row == 0, prev1, pltpu.roll(u, 1, axis=0))
    u2 = jnp.where(row == 0, prev0, jnp.where(row == 1, prev1, pltpu.roll(u, 2, axis=0)))
    _conv_finish(x_ref, mod_ref, b_gate, _conv_taps(u, u1, u2, wk_ref), wout_ref, o_ref)
    carry_ref[...] = u[tm - 2:, :]
    st_ref[...] = u[tm - 2:, :]


def _conv_step_kernel(x_ref, mod_ref, g_ref, win_ref, wk_ref, wout_ref, p1_ref, p2_ref, o_ref, u_ref, *, seq):
    tm = x_ref.shape[0]
    b_gate, u = _conv_gates(x_ref, mod_ref, g_ref, win_ref)
    t = lax.broadcasted_iota(jnp.int32, (tm, 1), 0) % seq
    u1 = jnp.where(t >= 1, pltpu.roll(u, 1, axis=0), p1_ref[...])
    u2 = jnp.where(t >= 2, pltpu.roll(u, 2, axis=0), p2_ref[...])
    _conv_finish(x_ref, mod_ref, b_gate, _conv_taps(u, u1, u2, wk_ref), wout_ref, o_ref)
    u_ref[...] = u


def _conv_weight_specs(d, n_taps, layer, j):
    return [
        _resident((None, None, 1, d), lambda i: (layer, 1, 0, 0)),
        _resident((None, d, 3 * d), lambda i: (j, 0, 0)),
        _resident((None, n_taps, d), lambda i: (j, 0, 0)),
        _resident((None, d, d), lambda i: (j, 0, 0)),
    ]


def _conv_prompt(x, mod, g_norm4, w_in, w_k, w_out, layer, j, n_seq):
    rows, d = x.shape
    seq = rows // n_seq
    assert w_k.shape[1] == 3 and seq >= 2
    tm = _tile(seq, 512)
    assert tm >= 2
    tiles_per_seq = seq // tm
    return pl.pallas_call(
        functools.partial(_conv_seq_kernel, tiles_per_seq=tiles_per_seq),
        grid=(rows // tm,),
        in_specs=[pl.BlockSpec((tm, d), lambda i: (i, 0)), _mod_spec(mod, layer, rows, tm)]
        + _conv_weight_specs(d, 3, layer, j),
        out_specs=[
            pl.BlockSpec((tm, d), lambda i: (i, 0)),
            pl.BlockSpec((None, 2, d), lambda i: (i // tiles_per_seq, 0, 0)),
        ],
        out_shape=[jax.ShapeDtypeStruct((rows, d), F32), jax.ShapeDtypeStruct((n_seq, 2, d), F32)],
        scratch_shapes=[pltpu.VMEM((2, d), F32)],
        compiler_params=_params("arbitrary"),
    )(x, mod, g_norm4, w_in, w_k, w_out)


def _conv_sample(x, mod, g_norm4, w_in, w_k, w_out, state, layer, j):
    rows, d = x.shape
    n_seq = state.shape[0]
    seq = rows // n_seq
    assert w_k.shape[1] == 3 and seq >= 2
    zeros = lambda n: jnp.zeros((n_seq, n, d), F32)
    prev1 = jnp.concatenate([state[:, 1:2], zeros(seq - 1)], axis=1).reshape(rows, d)
    prev2 = jnp.concatenate([state, zeros(seq - 2)], axis=1).reshape(rows, d)
    seqs_per_tile = _tile(n_seq, max(1, 512 // seq))
    tm = seqs_per_tile * seq
    row_spec = pl.BlockSpec((tm, d), lambda i: (i, 0))
    x_new, u = pl.pallas_call(
        functools.partial(_conv_step_kernel, seq=seq),
        grid=(rows // tm,),
        in_specs=[row_spec, _mod_spec(mod, layer, rows, tm)] + _conv_weight_specs(d, 3, layer, j) + [row_spec, row_spec],
        out_specs=[row_spec, row_spec],
        out_shape=[jax.ShapeDtypeStruct((rows, d), F32)] * 2,
        compiler_params=_params("arbitrary"),
    )(x, mod, g_norm4, w_in, w_k, w_out, prev1, prev2)
    return x_new, u.reshape(n_seq, seq, d)[:, seq - 2:]


def _mla_proj_kernel(*refs, dims, absorb):
    q_lora, kv_lora, rope, n_heads, nope = dims
    if absorb:
        (x_ref, mod_ref, g_ref, win_ref, gq_ref, gkv_ref, wuq_ref, ck_ref, sk_ref, cq_ref, sq_ref, wukt_ref,
         ql_ref, qr_ref, ckv_ref, kr_ref) = refs
    else:
        (x_ref, mod_ref, g_ref, win_ref, gq_ref, gkv_ref, wuq_ref, ck_ref, sk_ref, wuk_ref, wuvt_ref,
         q_ref, k_ref, vt_ref, ckv_ref, kr_ref) = refs
    def rope_pair(a):
        return a * ck_ref[...] + pltpu.roll(a, LANES // 2, axis=1) * sk_ref[...]

    h = _modulated(x_ref[...], g_ref[...], mod_ref, 1).astype(BF16)
    dn = _dot(h, win_ref[...])
    cq = _rmsnorm(dn[:, :q_lora], gq_ref[...]).astype(BF16)
    ckv = _rmsnorm(dn[:, q_lora:q_lora + kv_lora], gkv_ref[...])
    ckv_ref[...] = ckv
    o = q_lora + kv_lora
    krp = rope_pair(dn[:, o:o + LANES])
    kr_ref[...] = krp[:, :rope]
    q = _dot(cq, wuq_ref[...])
    hn = n_heads * nope
    if absorb:
        hr = n_heads * rope
        qr_ref[...] = (q[:, hn:hn + hr] * cq_ref[...] + q[:, hn + hr:hn + 2 * hr] * sq_ref[...]).astype(BF16)
        for i in range(n_heads):
            qn_i = q[:, i * nope:(i + 1) * nope].astype(BF16)
            ql_ref[:, i * kv_lora:(i + 1) * kv_lora] = _dot(qn_i, wukt_ref[i]).astype(BF16)
    else:
        ckv_b = ckv.astype(BF16)
        kn = _dot(ckv_b, wuk_ref[...])
        krp_b = krp.astype(BF16)
        w = nope + LANES
        for i in range(n_heads):
            q_ref[:, i * w:i * w + nope] = q[:, i * nope:(i + 1) * nope].astype(BF16)
            q_ref[:, i * w + nope:(i + 1) * w] = rope_pair(q[:, hn + i * LANES:hn + (i + 1) * LANES]).astype(BF16)
            k_ref[:, i * w:i * w + nope] = kn[:, i * nope:(i + 1) * nope].astype(BF16)
            k_ref[:, i * w + nope:(i + 1) * w] = krp_b
        vt_ref[...] = _dot_nt(wuvt_ref[...], ckv_b).astype(BF16)


def _rope_tables(pos, rope, width, reps):
    half = rope // 2
    inv = ROPE_THETA ** (-jnp.arange(half, dtype=F32) * (2.0 / rope))
    ang = pos.astype(F32)[:, None] * inv[None, :]
    cos, sin = jnp.cos(ang), jnp.sin(ang)
    pad = jnp.zeros((pos.shape[0], width - rope), F32)
    c = jnp.concatenate([cos, cos, pad], axis=1)
    s = jnp.concatenate([-sin, sin, pad], axis=1)
    return jnp.tile(c, (1, reps)), jnp.tile(s, (1, reps))


def _swap_halves(w, rope):
    half = rope // 2
    return jnp.concatenate([w[..., half:], w[..., :half]], axis=-1)


def _mla_weights(w_in, w_uq, q_lora, kv_lora, rope, nope, absorb):
    assert 2 * rope == LANES
    n_heads = w_uq.shape[1]
    w_kr = w_in[:, q_lora + kv_lora:]
    w_in_ext = jnp.concatenate([w_in[:, :q_lora + kv_lora], w_kr, _swap_halves(w_kr, rope)], axis=1)
    w_qn = w_uq[:, :, :nope].reshape(q_lora, n_heads * nope)
    w_qr = w_uq[:, :, nope:]
    w_qr_swapped = _swap_halves(w_qr, rope)
    if absorb:
        w_rope = [w_qr.reshape(q_lora, n_heads * rope), w_qr_swapped.reshape(q_lora, n_heads * rope)]
    else:
        w_rope = [jnp.concatenate([w_qr, w_qr_swapped], axis=-1).reshape(q_lora, n_heads * LANES)]
    w_uq_ext = jnp.concatenate([w_qn] + w_rope, axis=1)
    return w_in_ext.astype(BF16), w_uq_ext.astype(BF16)


def _mla_project(x, mod, g_norm4, layer, w_in, g_q, g_kv, w_uq, w_uk, w_uv, pos_rows, rows_per_pos_table, absorb):
    rows, d = x.shape
    q_lora, kv_lora = g_q.shape[0], g_kv.shape[0]
    rope = w_in.shape[1] - q_lora - kv_lora
    n_heads, nope = w_uk.shape[1], w_uk.shape[2]
    v_dim = w_uv.shape[2]
    w_in_ext, w_uq_ext = _mla_weights(w_in, w_uq, q_lora, kv_lora, rope, nope, absorb)
    cos_k, sin_k = _rope_tables(pos_rows, rope, LANES, 1)
    period = rows_per_pos_table
    tm = _tile(period, 512)
    tiles_per_period = period // tm
    row = lambda w: pl.BlockSpec((tm, w), lambda i: (i, 0))
    tab = lambda w: pl.BlockSpec((tm, w), lambda i: (i % tiles_per_period, 0))
    full = lambda a: _resident(a.shape, lambda i: (0,) * a.ndim)
    hn = n_heads * nope
    in_specs = [
        row(d), _mod_spec(mod, layer, rows, tm),
        _resident((None, None, 1, d), lambda i: (layer, 1, 0, 0)),
        full(w_in_ext), _resident((1, q_lora), lambda i: (0, 0)), _resident((1, kv_lora), lambda i: (0, 0)),
        full(w_uq_ext), tab(LANES), tab(LANES),
    ]
    args = [x, mod, g_norm4, w_in_ext, g_q.reshape(1, q_lora), g_kv.reshape(1, kv_lora), w_uq_ext, cos_k, sin_k]
    if absorb:
        hr = n_heads * rope
        cos_q, sin_q = _rope_tables(pos_rows, rope, rope, n_heads)
        w_ukt = jnp.transpose(w_uk, (1, 2, 0)).astype(BF16)
        in_specs += [tab(hr), tab(hr), full(w_ukt)]
        args += [cos_q, sin_q, w_ukt]
        outs = [(n_heads * kv_lora, BF16), (hr, BF16)]
    else:
        qk_w = n_heads * (nope + LANES)
        hv = n_heads * v_dim
        w_uk2 = w_uk.reshape(kv_lora, hn).astype(BF16)
        w_uvt = w_uv.reshape(kv_lora, hv).T.astype(BF16)
        in_specs += [full(w_uk2), full(w_uvt)]
        args += [w_uk2, w_uvt]
        outs = [(qk_w, BF16), (qk_w, BF16)]
    out_specs = [row(w) for w, _ in outs]
    out_shape = [jax.ShapeDtypeStruct((rows, w), dt) for w, dt in outs]
    if not absorb:
        out_specs.append(pl.BlockSpec((None, None, hv, tm), lambda i: (i // tiles_per_period, i % tiles_per_period, 0, 0)))
        out_shape.append(jax.ShapeDtypeStruct((rows // period, tiles_per_period, hv, tm), BF16))
    out_specs += [row(kv_lora), row(rope)]
    out_shape += [jax.ShapeDtypeStruct((rows, kv_lora), F32), jax.ShapeDtypeStruct((rows, rope), F32)]
    return pl.pallas_call(
        functools.partial(_mla_proj_kernel, dims=(q_lora, kv_lora, rope, n_heads, nope), absorb=absorb),
        grid=(rows // tm,),
        in_specs=in_specs,
        out_specs=out_specs,
        out_shape=out_shape,
        compiler_params=_params("arbitrary"),
    )(*args)


def _attn_kernel(q_ref, k_ref, vt_ref, x_ref, mod_ref, wo_ref, o_ref, oh_ref,
                 *, n_heads, heads_per_step, v_dim, scale):
    tq = q_ref.shape[0]
    tk = vt_ref.shape[-1]
    w = q_ref.shape[1] // n_heads
    q0 = pl.program_id(1) * tq
    n_full = q0 // tk
    n_kv = (q0 + tq + tk - 1) // tk
    key0 = lax.broadcasted_iota(jnp.int32, (tk, 1), 0)
    query = q0 + lax.broadcasted_iota(jnp.int32, (1, tq), 1)

    log2e_scale = scale * LOG2_E

    def step(c, carry, heads, masked):
        off = pl.multiple_of(c * tk, tk)
        scores = []
        for i in heads:
            s = _dot_nt(k_ref[pl.ds(off, tk), i * w:(i + 1) * w], q_ref[:, i * w:(i + 1) * w])
            scores.append(jnp.where(query >= off + key0, s, -jnp.inf) if masked else s)
        out = []
        for i, s, (m, l, acc) in zip(heads, scores, carry):
            m_new = jnp.maximum(m, jnp.max(s, axis=0, keepdims=True) * log2e_scale)
            alpha = jnp.exp2(m - m_new)
            p = jnp.exp2(s * log2e_scale - m_new)
            l = alpha * l + jnp.sum(p, axis=0, keepdims=True)
            acc = alpha * acc + _dot(vt_ref[c, i * v_dim:(i + 1) * v_dim, :], p.astype(BF16))
            out.append((m_new, l, acc))
        return tuple(out)

    init = (jnp.full((1, tq), NEG_BIG, F32), jnp.zeros((1, tq), F32), jnp.zeros((v_dim, tq), F32))
    for g in range(0, n_heads, heads_per_step):
        heads = tuple(range(g, g + heads_per_step))
        carry = lax.fori_loop(0, n_full, functools.partial(step, heads=heads, masked=False), (init,) * heads_per_step)
        carry = lax.fori_loop(n_full, n_kv, functools.partial(step, heads=heads, masked=True), carry)
        for i, (_, l, acc) in zip(heads, carry):
            oh_ref[:, i * v_dim:(i + 1) * v_dim] = (acc / l).T.astype(BF16)
    o_ref[...] = x_ref[...] + _mod(mod_ref, MIXER_GATE) * _dot(oh_ref[...], wo_ref[...])


def _attn_prompt(q, k, vt, x, mod, w_o, layer, j, n_seq, n_heads, scale):
    rows, d = x.shape
    seq = rows // n_seq
    _, n_kv, hv, tk = vt.shape
    assert n_kv * tk == seq
    tq = _tile(seq, 512)
    r3 = lambda a: a.reshape(n_seq, seq, a.shape[-1])
    q_spec = lambda w: pl.BlockSpec((None, tq, w), lambda b, i: (b, i, 0))
    out = pl.pallas_call(
        functools.partial(_attn_kernel, n_heads=n_heads, heads_per_step=_tile(n_heads, 4),
                          v_dim=hv // n_heads, scale=scale),
        grid=(n_seq, seq // tq),
        in_specs=[
            q_spec(q.shape[-1]),
            pl.BlockSpec((None, seq, k.shape[-1]), lambda b, i: (b, 0, 0)),
            pl.BlockSpec((None, n_kv, hv, tk), lambda b, i: (b, 0, 0, 0)),
            q_spec(d),
            pl.BlockSpec((None, None, 1, mod.shape[-1]), lambda b, i: (layer, b, 0, 0)),
            _resident((None, hv, d), lambda b, i: (j, 0, 0)),
        ],
        out_specs=q_spec(d),
        out_shape=jax.ShapeDtypeStruct((n_seq, seq, d), F32),
        scratch_shapes=[pltpu.VMEM((tq, hv), BF16)],
        compiler_params=_params("arbitrary", "arbitrary"),
    )(r3(q), r3(k), vt, r3(x), mod, w_o)
    return out.reshape(rows, d)


def _decode_kernel(pt_ref, ql_ref, qr_ref, cn_ref, kn_ref, lat_hbm, krt_hbm, o_ref, lat_buf, krt_buf, sem,
                   *, layer, n_chunks, pages_per_chunk, page, n_heads, n_new, n_split, scale):
    b = pl.program_id(0)
    n_b = pl.num_programs(0)
    n_rows = ql_ref.shape[0]

    def copies(bb, c, slot):
        out = []
        for k in range(pages_per_chunk):
            pg = pt_ref[bb, c * pages_per_chunk + k]
            dst = pl.ds(k * page, page)
            out.append(pltpu.make_async_copy(lat_hbm.at[layer, pg], lat_buf.at[slot, dst], sem.at[0, slot]))
            out.append(pltpu.make_async_copy(krt_hbm.at[layer, pg], krt_buf.at[slot, :, dst], sem.at[1, slot]))
        return out

    def start(bb, c, slot):
        for n, cp in enumerate(copies(bb, c, slot)):
            cp.start(priority=(n // 2) % 2)

    @pl.when(b == 0)
    def _():
        start(0, 0, 0)

    ql = ql_ref[...]
    qr = qr_ref[...]

    def online(carry, s, values):
        m, l, acc = carry
        m_new = jnp.maximum(m, jnp.max(s, axis=-1, keepdims=True))
        alpha = jnp.exp(m - m_new)
        p = jnp.exp(s - m_new)
        l = alpha * l + jnp.sum(p, axis=-1, keepdims=True)
        acc = alpha * acc + _dot(p.astype(BF16), values)
        return m_new, l, acc

    def step(c, carry):
        slot = (b * n_chunks + c) % 2
        last = c + 1 == n_chunks
        nb = jnp.where(last, b + 1, b)
        nc = jnp.where(last, 0, c + 1)

        @pl.when(nb < n_b)
        def _():
            start(nb, nc, 1 - slot)

        for cp in copies(b, c, slot):
            cp.wait()
        lats, scores = [], []
        for k in range(n_split):
            keys = slice(k * sub, (k + 1) * sub)
            lats.append(lat_buf[slot, keys, :].astype(BF16))
            krt = krt_buf[slot, :, keys].astype(BF16)
            scores.append((_dot_nt(ql, lats[k]) + _dot(qr, krt)) * scale)
        return tuple(online(state, s, lat) for state, s, lat in zip(carry, scores, lats))

    sub = pages_per_chunk * page // n_split
    init = (jnp.full((n_rows, 1), NEG_BIG, F32), jnp.zeros((n_rows, 1), F32),
            jnp.zeros((n_rows, lat_buf.shape[-1]), F32))
    states = lax.fori_loop(0, n_chunks, step, (init,) * n_split)
    m = functools.reduce(jnp.maximum, [st[0] for st in states])
    weights = [jnp.exp(st[0] - m) for st in states]
    carry = (m, sum(w * st[1] for w, st in zip(weights, states)), sum(w * st[2] for w, st in zip(weights, states)))

    cn = cn_ref[...]
    s_new = (_dot_nt(ql, cn) + _dot_nt(qr, kn_ref[...])) * scale
    t = lax.broadcasted_iota(jnp.int32, (n_rows, 1), 0) // n_heads
    u = lax.broadcasted_iota(jnp.int32, (1, cn.shape[0]), 1)
    s_new = jnp.where((u <= t) & (u < n_new), s_new, -jnp.inf)
    _, l, acc = online(carry, s_new, cn)
    o_ref[...] = acc / l


def _decode_attn(page_table, q_lat, q_rope, ckv_new, kr_new, cache_lat, cache_krt, layer, n_heads, scale):
    n_b, n_rows, kv_lora = q_lat.shape
    rope = q_rope.shape[-1]
    n_new = n_rows // n_heads
    page = cache_lat.shape[2]
    n_pages = page_table.shape[1]
    pages_per_chunk = _tile(n_pages, 32)
    n_split = _tile(pages_per_chunk, 4)
    n_chunks = n_pages // pages_per_chunk
    key_pad = LANES
    assert n_new <= key_pad
    pad_keys = lambda a: jnp.pad(a.astype(BF16), ((0, 0), (0, key_pad - n_new), (0, 0)))
    per_b = lambda r, w: pl.BlockSpec((None, r, w), lambda b, pt: (b, 0, 0))
    return pl.pallas_call(
        functools.partial(_decode_kernel, layer=layer, n_chunks=n_chunks, pages_per_chunk=pages_per_chunk,
                          page=page, n_heads=n_heads, n_new=n_new, n_split=n_split,
                          scale=scale),
        grid_spec=pltpu.PrefetchScalarGridSpec(
            num_scalar_prefetch=1,
            grid=(n_b,),
            in_specs=[
                per_b(n_rows, kv_lora), per_b(n_rows, rope), per_b(key_pad, kv_lora), per_b(key_pad, rope),
                pl.BlockSpec(memory_space=pl.ANY), pl.BlockSpec(memory_space=pl.ANY),
            ],
            out_specs=per_b(n_rows, kv_lora),
            scratch_shapes=[
                pltpu.VMEM((2, pages_per_chunk * page, kv_lora), F32),
                pltpu.VMEM((2, rope, pages_per_chunk * page), F32),
                pltpu.SemaphoreType.DMA((2, 2)),
            ],
        ),
        out_shape=jax.ShapeDtypeStruct((n_b, n_rows, kv_lora), F32),
        compiler_params=_params("arbitrary"),
    )(page_table, q_lat, q_rope, pad_keys(ckv_new), pad_keys(kr_new), cache_lat, cache_krt)


def _decode_out_kernel(ol_ref, x_ref, mod_ref, wuv_ref, wo_ref, o_ref, oh_ref, *, n_heads, kv_lora, v_dim):
    for i in range(n_heads):
        o_i = ol_ref[:, i * kv_lora:(i + 1) * kv_lora].astype(BF16)
        oh_ref[:, i * v_dim:(i + 1) * v_dim] = _dot(o_i, wuv_ref[i]).astype(BF16)
    o_ref[...] = x_ref[...] + _mod(mod_ref, MIXER_GATE) * _dot(oh_ref[...], wo_ref[...])


def _decode_out(o_lat, x, mod, w_uv, w_o, layer, j):
    rows, d = x.shape
    kv_lora, n_heads, v_dim = w_uv.shape
    w_uv_h = jnp.transpose(w_uv, (1, 0, 2)).astype(BF16)
    tm = _tile(rows, 512)
    row = lambda w: pl.BlockSpec((tm, w), lambda i: (i, 0))
    return pl.pallas_call(
        functools.partial(_decode_out_kernel, n_heads=n_heads, kv_lora=kv_lora, v_dim=v_dim),
        grid=(rows // tm,),
        in_specs=[
            row(n_heads * kv_lora), row(d), _mod_spec(mod, layer, rows, tm),
            _resident(w_uv_h.shape, lambda i: (0, 0, 0)),
            _resident((None, n_heads * v_dim, d), lambda i: (j, 0, 0)),
        ],
        out_specs=row(d),
        out_shape=jax.ShapeDtypeStruct((rows, d), F32),
        scratch_shapes=[pltpu.VMEM((tm, n_heads * v_dim), BF16)],
        compiler_params=_params("arbitrary"),
    )(o_lat, x, mod, w_uv_h, w_o)


def kernel(x_prompt, x_sample, c_prompt, c_sample, state_conv, cache_kv_latent, cache_k_rope, page_table, w_ada, b_ada, g_norm, w_ffn_gu, w_ffn_down, w_conv_in, w_conv_k, w_conv_out, w_mla_in, g_q_norm, g_kv_norm, w_uq, w_uk, w_uv, w_mla_out, g_final):
    n_p, seq_p, d = x_prompt.shape
    n_s, seq_s, _ = x_sample.shape
    depth = w_ada.shape[0]
    past = page_table.shape[1] * cache_kv_latent.shape[2]
    n_heads, qk_dim = w_uq.shape[2], w_uq.shape[3]
    v_dim = w_uv.shape[3]
    kv_lora = g_kv_norm.shape[1]
    scale = qk_dim ** -0.5

    xp = x_prompt.reshape(n_p * seq_p, d)
    xs = x_sample.reshape(n_s * seq_s, d)
    mp, ms = _adaln(c_prompt, jnp.repeat(c_sample, seq_s, axis=0), w_ada, b_ada)
    mp = mp.reshape(depth, n_p, 1, N_MOD * d)
    g_norm4 = g_norm.reshape(depth, N_SUB, 1, d)
    w_gu = w_ffn_gu.astype(BF16)
    w_down = w_ffn_down.astype(BF16)
    w_cin = w_conv_in.astype(BF16)
    w_cout = w_conv_out.astype(BF16)
    w_mo = w_mla_out.astype(BF16)
    cache_krt = jnp.swapaxes(cache_k_rope, 2, 3)

    conv_p, conv_s, lat_p, kr_p, lat_s, kr_s = [], [], [], [], [], []
    for i in range(depth):
        last = i == depth - 1
        xp = _ffn(xp, mp, g_norm4, w_gu, w_down, i, 0, 0)
        xs = _ffn(xs, ms, g_norm4, w_gu, w_down, i, 0, 0)
        j = i // N_MIXERS
        if i % N_MIXERS == 0:
            xp, st_p = _conv_prompt(xp, mp, g_norm4, w_cin, w_conv_k, w_cout, i, j, n_p)
            xs, st_s = _conv_sample(xs, ms, g_norm4, w_cin, w_conv_k, w_cout, state_conv[j], i, j)
            conv_p.append(st_p)
            conv_s.append(st_s)
        else:
            proj = (w_mla_in[j], g_q_norm[j], g_kv_norm[j], w_uq[j], w_uk[j], w_uv[j])
            q, k, vt, ckv_p, k_p = _mla_project(
                xp, mp, g_norm4, i, *proj, jnp.arange(seq_p, dtype=jnp.int32), seq_p, absorb=False)
            xp = _attn_prompt(q, k, vt, xp, mp, w_mo, i, j, n_p, n_heads, scale)
            pos_s = jnp.tile(past + jnp.arange(seq_s, dtype=jnp.int32), n_s)
            ql, qrs, ckv_s, k_s = _mla_project(xs, ms, g_norm4, i, *proj, pos_s, n_s * seq_s, absorb=True)
            o_lat = _decode_attn(
                page_table, ql.reshape(n_s, seq_s * n_heads, kv_lora), qrs.reshape(n_s, seq_s * n_heads, -1),
                ckv_s.reshape(n_s, seq_s, -1), k_s.reshape(n_s, seq_s, -1),
                cache_kv_latent, cache_krt, j, n_heads, scale)
            xs = _decode_out(o_lat.reshape(n_s * seq_s, n_heads * kv_lora), xs, ms, w_uv[j], w_mo, i, j)
            lat_p.append(ckv_p.reshape(n_p, seq_p, -1))
            kr_p.append(k_p.reshape(n_p, seq_p, -1))
            lat_s.append(ckv_s.reshape(n_s, seq_s, -1))
            kr_s.append(k_s.reshape(n_s, seq_s, -1))
        xp = _ffn(xp, mp, g_norm4, w_gu, w_down, i, 1, 2, g_final if last else None)
        xs = _ffn(xs, ms, g_norm4, w_gu, w_down, i, 1, 2, g_final if last else None)
    return (xp.reshape(n_p, seq_p, d), xs.reshape(n_s, seq_s, d), jnp.stack(conv_p), jnp.stack(conv_s),
            jnp.stack(lat_p), jnp.stack(kr_p), jnp.stack(lat_s), jnp.stack(kr_s))
```

```python
import functools

import jax
import jax.numpy as jnp
from jax import lax
from jax.experimental import pallas as pl
from jax.experimental.pallas import tpu as pltpu

EPS = 1e-6
ROPE_THETA = 10000.0
N_SUB = 3
N_MOD = 3 * N_SUB
N_MIXERS = 2
MIXER_GATE = 3 * 1 + 2

BF16 = jnp.bfloat16
F32 = jnp.float32

LANES = 128
VMEM_LIMIT_BYTES = 56 * 1024 * 1024
NEG_BIG = -1e30
LOG2_E = 1.4426950408889634


def _params(*semantics):
    return pltpu.CompilerParams(dimension_semantics=semantics, vmem_limit_bytes=VMEM_LIMIT_BYTES)


def _dot(a, b):
    return jnp.dot(a, b, preferred_element_type=F32)


def _dot_nt(a, b):
    return lax.dot_general(a, b, (((1,), (1,)), ((), ())), preferred_element_type=F32)


def _rmsnorm(x, g):
    return x * lax.rsqrt(jnp.mean(x * x, axis=-1, keepdims=True) + EPS) * g


def _mod(mod_ref, k):
    d = mod_ref.shape[-1] // N_MOD
    return mod_ref[:, k * d:(k + 1) * d]


def _modulated(x, g, mod_ref, s):
    return _rmsnorm(x, g) * (1.0 + _mod(mod_ref, 3 * s + 1)) + _mod(mod_ref, 3 * s)


def _tile(n, want):
    t = min(n, want)
    while n % t:
        t -= 1
    return t


def _resident(block_shape, index_map):
    return pl.BlockSpec(block_shape, index_map, pipeline_mode=pl.Buffered(1))


def _rows_per_mod(mod, rows):
    return rows // mod.shape[1] if mod.ndim == 4 else 1


def _mod_spec(mod, layer, rows, tm):
    width = mod.shape[-1]
    if mod.ndim == 4:
        rows_per_seq = _rows_per_mod(mod, rows)
        assert rows_per_seq % tm == 0
        tiles_per_seq = rows_per_seq // tm
        return pl.BlockSpec((None, None, 1, width), lambda i, *_: (layer, i // tiles_per_seq, 0, 0))
    assert mod.shape[1] == rows
    return pl.BlockSpec((None, tm, width), lambda i, *_: (layer, i, 0))


def _adaln_kernel(cp_ref, cs_ref, w_ref, b_ref, op_ref, os_ref):
    w = w_ref[...].astype(BF16)
    for c_ref, o_ref in ((cp_ref, op_ref), (cs_ref, os_ref)):
        c = c_ref[...]
        o_ref[...] = _dot((c * jax.nn.sigmoid(c)).astype(BF16), w) + b_ref[...]


def _adaln(c_p, c_s, w_ada, b_ada):
    depth, d, n_out = w_ada.shape
    tn = _tile(n_out, 1024)
    c_spec = lambda c: _resident(c.shape, lambda l, j: (0, 0))
    o_spec = lambda c: pl.BlockSpec((None, c.shape[0], tn), lambda l, j: (l, 0, j))
    return pl.pallas_call(
        _adaln_kernel,
        grid=(depth, n_out // tn),
        in_specs=[
            c_spec(c_p), c_spec(c_s),
            pl.BlockSpec((None, d, tn), lambda l, j: (l, 0, j)),
            pl.BlockSpec((None, 1, tn), lambda l, j: (l, 0, j)),
        ],
        out_specs=[o_spec(c_p), o_spec(c_s)],
        out_shape=[jax.ShapeDtypeStruct((depth, c.shape[0], n_out), F32) for c in (c_p, c_s)],
        compiler_params=_params("arbitrary", "arbitrary"),
    )(c_p, c_s, w_ada, b_ada.reshape(depth, 1, n_out))


def _ffn_kernel(*refs, s, tf, final_norm):
    if final_norm:
        x_ref, mod_ref, g_ref, wgu_ref, wd_ref, gf_ref, o_ref, h_ref, a_ref = refs
    else:
        x_ref, mod_ref, g_ref, wgu_ref, wd_ref, o_ref, h_ref, a_ref = refs
    f = wd_ref.shape[0]
    h_ref[...] = _modulated(x_ref[...], g_ref[...], mod_ref, s).astype(BF16)
    for c in range(f // tf):
        gate = _dot(h_ref[...], wgu_ref[:, c * tf:(c + 1) * tf])
        up = _dot(h_ref[...], wgu_ref[:, f + c * tf:f + (c + 1) * tf])
        a_ref[:, c * tf:(c + 1) * tf] = (gate * jax.nn.sigmoid(gate) * up).astype(BF16)
    xn = x_ref[...] + 0.5 * _mod(mod_ref, 3 * s + 2) * _dot(a_ref[...], wd_ref[...])
    if final_norm:
        xn = _rmsnorm(xn, gf_ref[...])
    o_ref[...] = xn


def _ffn(x, mod, g_norm4, w_gu, w_down, layer, k, s, g_final=None):
    rows, d = x.shape
    f = w_down.shape[2]
    tm = _tile(rows if mod.ndim == 3 else _rows_per_mod(mod, rows), 512)
    tf = _tile(f, 256)
    in_specs = [
        pl.BlockSpec((tm, d), lambda i: (i, 0)),
        _mod_spec(mod, layer, rows, tm),
        _resident((None, None, 1, d), lambda i: (layer, s, 0, 0)),
        _resident((None, None, d, 2 * f), lambda i: (layer, k, 0, 0)),
        _resident((None, None, f, d), lambda i: (layer, k, 0, 0)),
    ]
    args = [x, mod, g_norm4, w_gu, w_down]
    if g_final is not None:
        in_specs.append(_resident((1, d), lambda i: (0, 0)))
        args.append(g_final.reshape(1, d))
    return pl.pallas_call(
        functools.partial(_ffn_kernel, s=s, tf=tf, final_norm=g_final is not None),
        grid=(rows // tm,),
        in_specs=in_specs,
        out_specs=pl.BlockSpec((tm, d), lambda i: (i, 0)),
        out_shape=jax.ShapeDtypeStruct((rows, d), F32),
        scratch_shapes=[pltpu.VMEM((tm, d), BF16), pltpu.VMEM((tm, f), BF16)],
        compiler_params=_params("arbitrary"),
    )(*args)


CONV_COLS = 256


def _conv_mixer(x_ref, mod_ref, g_ref, win_ref, wk_ref, wout_ref, o_ref, h_ref, bc_ref, shifted, keep):
    d = x_ref.shape[-1]
    h_ref[...] = _modulated(x_ref[...], g_ref[...], mod_ref, 1).astype(BF16)
    for c in range(0, d, CONV_COLS):
        cols = slice(c, c + CONV_COLS)
        proj = lambda part: _dot(h_ref[...], win_ref[:, part * d + c:part * d + c + CONV_COLS])
        b_gate = proj(0)
        u = proj(1) * proj(2)
        u1, u2 = shifted(u, cols)
        conv = wk_ref[0:1, cols] * u2 + wk_ref[1:2, cols] * u1 + wk_ref[2:3, cols] * u
        bc_ref[:, cols] = (b_gate * conv).astype(BF16)
        keep(u, cols)
    o_ref[...] = x_ref[...] + _mod(mod_ref, MIXER_GATE) * _dot(bc_ref[...], wout_ref[...])


def _conv_seq_kernel(x_ref, mod_ref, g_ref, win_ref, wk_ref, wout_ref, o_ref, st_ref, h_ref, bc_ref, carry_ref,
                     *, tiles_per_seq):
    tm = x_ref.shape[0]

    @pl.when(pl.program_id(0) % tiles_per_seq == 0)
    def _():
        carry_ref[...] = jnp.zeros_like(carry_ref)

    row = lax.broadcasted_iota(jnp.int32, (tm, 1), 0)

    def shifted(u, cols):
        prev0 = carry_ref[0:1, cols]
        prev1 = carry_ref[1:2, cols]
        u1 = jnp.where(row == 0, prev1, pltpu.roll(u, 1, axis=0))
        u2 = jnp.where(row == 0, prev0, jnp.where(row == 1, prev1, pltpu.roll(u, 2, axis=0)))
        return u1, u2

    def keep(u, cols):
        carry_ref[:, cols] = u[tm - 2:, :]
        st_ref[:, cols] = u[tm - 2:, :]

    _conv_mixer(x_ref, mod_ref, g_ref, win_ref, wk_ref, wout_ref, o_ref, h_ref, bc_ref, shifted, keep)


def _conv_step_kernel(x_ref, mod_ref, g_ref, win_ref, wk_ref, wout_ref, p1_ref, p2_ref, o_ref, u_ref, h_ref, bc_ref,
                      *, seq):
    tm = x_ref.shape[0]
    t = lax.broadcasted_iota(jnp.int32, (tm, 1), 0) % seq

    def shifted(u, cols):
        u1 = jnp.where(t >= 1, pltpu.roll(u, 1, axis=0), p1_ref[:, cols])
        u2 = jnp.where(t >= 2, pltpu.roll(u, 2, axis=0), p2_ref[:, cols])
        return u1, u2

    def keep(u, cols):
        u_ref[:, cols] = u

    _conv_mixer(x_ref, mod_ref, g_ref, win_ref, wk_ref, wout_ref, o_ref, h_ref, bc_ref, shifted, keep)


def _conv_scratch(tm, d):
    assert d % CONV_COLS == 0
    return [pltpu.VMEM((tm, d), BF16), pltpu.VMEM((tm, d), BF16)]


def _conv_weight_specs(d, n_taps, layer, j):
    return [
        _resident((None, None, 1, d), lambda i: (layer, 1, 0, 0)),
        _resident((None, d, 3 * d), lambda i: (j, 0, 0)),
        _resident((None, n_taps, d), lambda i: (j, 0, 0)),
        _resident((None, d, d), lambda i: (j, 0, 0)),
    ]


def _conv_prompt(x, mod, g_norm4, w_in, w_k, w_out, layer, j, n_seq):
    rows, d = x.shape
    seq = rows // n_seq
    assert w_k.shape[1] == 3 and seq >= 2
    tm = _tile(seq, 512)
    assert tm >= 2
    tiles_per_seq = seq // tm
    return pl.pallas_call(
        functools.partial(_conv_seq_kernel, tiles_per_seq=tiles_per_seq),
        grid=(rows // tm,),
        in_specs=[pl.BlockSpec((tm, d), lambda i: (i, 0)), _mod_spec(mod, layer, rows, tm)]
        + _conv_weight_specs(d, 3, layer, j),
        out_specs=[
            pl.BlockSpec((tm, d), lambda i: (i, 0)),
            pl.BlockSpec((None, 2, d), lambda i: (i // tiles_per_seq, 0, 0)),
        ],
        out_shape=[jax.ShapeDtypeStruct((rows, d), F32), jax.ShapeDtypeStruct((n_seq, 2, d), F32)],
        scratch_shapes=_conv_scratch(tm, d) + [pltpu.VMEM((2, d), F32)],
        compiler_params=_params("arbitrary"),
    )(x, mod, g_norm4, w_in, w_k, w_out)


def _conv_sample(x, mod, g_norm4, w_in, w_k, w_out, state, layer, j):
    rows, d = x.shape
    n_seq = state.shape[0]
    seq = rows // n_seq
    assert w_k.shape[1] == 3 and seq >= 2
    zeros = lambda n: jnp.zeros((n_seq, n, d), F32)
    prev1 = jnp.concatenate([state[:, 1:2], zeros(seq - 1)], axis=1).reshape(rows, d)
    prev2 = jnp.concatenate([state, zeros(seq - 2)], axis=1).reshape(rows, d)
    seqs_per_tile = _tile(n_seq, max(1, 512 // seq))
    tm = seqs_per_tile * seq
    row_spec = pl.BlockSpec((tm, d), lambda i: (i, 0))
    x_new, u = pl.pallas_call(
        functools.partial(_conv_step_kernel, seq=seq),
        grid=(rows // tm,),
        in_specs=[row_spec, _mod_spec(mod, layer, rows, tm)] + _conv_weight_specs(d, 3, layer, j) + [row_spec, row_spec],
        out_specs=[row_spec, row_spec],
        out_shape=[jax.ShapeDtypeStruct((rows, d), F32)] * 2,
        scratch_shapes=_conv_scratch(tm, d),
        compiler_params=_params("arbitrary"),
    )(x, mod, g_norm4, w_in, w_k, w_out, prev1, prev2)
    return x_new, u.reshape(n_seq, seq, d)[:, seq - 2:]


def _mla_proj_kernel(*refs, dims, absorb):
    q_lora, kv_lora, rope, n_heads, nope = dims
    if absorb:
        (x_ref, mod_ref, g_ref, win_ref, gq_ref, gkv_ref, wuq_ref, ck_ref, sk_ref, cq_ref, sq_ref, wukt_ref,
         ql_ref, qr_ref, ckv_ref, kr_ref) = refs
    else:
        (x_ref, mod_ref, g_ref, win_ref, gq_ref, gkv_ref, wuq_ref, ck_ref, sk_ref, wuk_ref, wuvt_ref,
         q_ref, k_ref, vt_ref, ckv_ref, kr_ref) = refs
    def rope_pair(a):
        return a * ck_ref[...] + pltpu.roll(a, LANES // 2, axis=1) * sk_ref[...]

    h = _modulated(x_ref[...], g_ref[...], mod_ref, 1).astype(BF16)
    dn = _dot(h, win_ref[...])
    cq = _rmsnorm(dn[:, :q_lora], gq_ref[...]).astype(BF16)
    ckv = _rmsnorm(dn[:, q_lora:q_lora + kv_lora], gkv_ref[...])
    ckv_ref[...] = ckv
    o = q_lora + kv_lora
    krp = rope_pair(dn[:, o:o + LANES])
    kr_ref[...] = krp[:, :rope]
    q = _dot(cq, wuq_ref[...])
    hn = n_heads * nope
    if absorb:
        hr = n_heads * rope
        qr_ref[...] = (q[:, hn:hn + hr] * cq_ref[...] + q[:, hn + hr:hn + 2 * hr] * sq_ref[...]).astype(BF16)
        for i in range(n_heads):
            qn_i = q[:, i * nope:(i + 1) * nope].astype(BF16)
            ql_ref[:, i * kv_lora:(i + 1) * kv_lora] = _dot(qn_i, wukt_ref[i]).astype(BF16)
    else:
        ckv_b = ckv.astype(BF16)
        kn = _dot(ckv_b, wuk_ref[...])
        krp_b = krp.astype(BF16)
        w = nope + LANES
        for i in range(n_heads):
            q_ref[:, i * w:i * w + nope] = q[:, i * nope:(i + 1) * nope].astype(BF16)
            q_ref[:, i * w + nope:(i + 1) * w] = rope_pair(q[:, hn + i * LANES:hn + (i + 1) * LANES]).astype(BF16)
            k_ref[:, i * w:i * w + nope] = kn[:, i * nope:(i + 1) * nope].astype(BF16)
            k_ref[:, i * w + nope:(i + 1) * w] = krp_b
        vt_ref[...] = _dot_nt(wuvt_ref[...], ckv_b).astype(BF16)


def _rope_tables(pos, rope, width, reps):
    half = rope // 2
    inv = ROPE_THETA ** (-jnp.arange(half, dtype=F32) * (2.0 / rope))
    ang = pos.astype(F32)[:, None] * inv[None, :]
    cos, sin = jnp.cos(ang), jnp.sin(ang)
    pad = jnp.zeros((pos.shape[0], width - rope), F32)
    c = jnp.concatenate([cos, cos, pad], axis=1)
    s = jnp.concatenate([-sin, sin, pad], axis=1)
    return jnp.tile(c, (1, reps)), jnp.tile(s, (1, reps))


def _swap_halves(w, rope):
    half = rope // 2
    return jnp.concatenate([w[..., half:], w[..., :half]], axis=-1)


def _mla_weights(w_in, w_uq, q_lora, kv_lora, rope, nope, absorb):
    assert 2 * rope == LANES
    n_heads = w_uq.shape[1]
    w_kr = w_in[:, q_lora + kv_lora:]
    w_in_ext = jnp.concatenate([w_in[:, :q_lora + kv_lora], w_kr, _swap_halves(w_kr, rope)], axis=1)
    w_qn = w_uq[:, :, :nope].reshape(q_lora, n_heads * nope)
    w_qr = w_uq[:, :, nope:]
    w_qr_swapped = _swap_halves(w_qr, rope)
    if absorb:
        w_rope = [w_qr.reshape(q_lora, n_heads * rope), w_qr_swapped.reshape(q_lora, n_heads * rope)]
    else:
        w_rope = [jnp.concatenate([w_qr, w_qr_swapped], axis=-1).reshape(q_lora, n_heads * LANES)]
    w_uq_ext = jnp.concatenate([w_qn] + w_rope, axis=1)
    return w_in_ext.astype(BF16), w_uq_ext.astype(BF16)


def _mla_project(x, mod, g_norm4, layer, w_in, g_q, g_kv, w_uq, w_uk, w_uv, pos_rows, rows_per_pos_table, absorb):
    rows, d = x.shape
    q_lora, kv_lora = g_q.shape[0], g_kv.shape[0]
    rope = w_in.shape[1] - q_lora - kv_lora
    n_heads, nope = w_uk.shape[1], w_uk.shape[2]
    v_dim = w_uv.shape[2]
    w_in_ext, w_uq_ext = _mla_weights(w_in, w_uq, q_lora, kv_lora, rope, nope, absorb)
    cos_k, sin_k = _rope_tables(pos_rows, rope, LANES, 1)
    period = rows_per_pos_table
    tm = _tile(period, 512)
    tiles_per_period = period // tm
    row = lambda w: pl.BlockSpec((tm, w), lambda i: (i, 0))
    tab = lambda w: pl.BlockSpec((tm, w), lambda i: (i % tiles_per_period, 0))
    full = lambda a: _resident(a.shape, lambda i: (0,) * a.ndim)
    hn = n_heads * nope
    in_specs = [
        row(d), _mod_spec(mod, layer, rows, tm),
        _resident((None, None, 1, d), lambda i: (layer, 1, 0, 0)),
        full(w_in_ext), _resident((1, q_lora), lambda i: (0, 0)), _resident((1, kv_lora), lambda i: (0, 0)),
        full(w_uq_ext), tab(LANES), tab(LANES),
    ]
    args = [x, mod, g_norm4, w_in_ext, g_q.reshape(1, q_lora), g_kv.reshape(1, kv_lora), w_uq_ext, cos_k, sin_k]
    if absorb:
        hr = n_heads * rope
        cos_q, sin_q = _rope_tables(pos_rows, rope, rope, n_heads)
        w_ukt = jnp.transpose(w_uk, (1, 2, 0)).astype(BF16)
        in_specs += [tab(hr), tab(hr), full(w_ukt)]
        args += [cos_q, sin_q, w_ukt]
        outs = [(n_heads * kv_lora, BF16), (hr, BF16)]
    else:
        qk_w = n_heads * (nope + LANES)
        hv = n_heads * v_dim
        w_uk2 = w_uk.reshape(kv_lora, hn).astype(BF16)
        w_uvt = w_uv.reshape(kv_lora, hv).T.astype(BF16)
        in_specs += [full(w_uk2), full(w_uvt)]
        args += [w_uk2, w_uvt]
        outs = [(qk_w, BF16), (qk_w, BF16)]
    out_specs = [row(w) for w, _ in outs]
    out_shape = [jax.ShapeDtypeStruct((rows, w), dt) for w, dt in outs]
    if not absorb:
        out_specs.append(pl.BlockSpec((None, None, hv, tm), lambda i: (i // tiles_per_period, i % tiles_per_period, 0, 0)))
        out_shape.append(jax.ShapeDtypeStruct((rows // period, tiles_per_period, hv, tm), BF16))
    out_specs += [row(kv_lora), row(rope)]
    out_shape += [jax.ShapeDtypeStruct((rows, kv_lora), F32), jax.ShapeDtypeStruct((rows, rope), F32)]
    return pl.pallas_call(
        functools.partial(_mla_proj_kernel, dims=(q_lora, kv_lora, rope, n_heads, nope), absorb=absorb),
        grid=(rows // tm,),
        in_specs=in_specs,
        out_specs=out_specs,
        out_shape=out_shape,
        compiler_params=_params("arbitrary"),
    )(*args)


def _attn_kernel(q_ref, k_ref, vt_ref, x_ref, mod_ref, wo_ref, o_ref, oh_ref,
                 *, n_heads, heads_per_step, v_dim, scale):
    tq = q_ref.shape[0]
    tk = vt_ref.shape[-1]
    w = q_ref.shape[1] // n_heads
    q0 = pl.program_id(1) * tq
    n_full = q0 // tk
    n_kv = (q0 + tq + tk - 1) // tk
    key0 = lax.broadcasted_iota(jnp.int32, (tk, 1), 0)
    query = q0 + lax.broadcasted_iota(jnp.int32, (1, tq), 1)

    log2e_scale = scale * LOG2_E

    def step(c, carry, heads, masked):
        off = pl.multiple_of(c * tk, tk)
        scores = []
        for i in heads:
            s = _dot_nt(k_ref[pl.ds(off, tk), i * w:(i + 1) * w], q_ref[:, i * w:(i + 1) * w])
            scores.append(jnp.where(query >= off + key0, s, -jnp.inf) if masked else s)
        out = []
        for i, s, (m, l, acc) in zip(heads, scores, carry):
            m_new = jnp.maximum(m, jnp.max(s, axis=0, keepdims=True) * log2e_scale)
            alpha = jnp.exp2(m - m_new)
            p = jnp.exp2(s * log2e_scale - m_new)
            l = alpha * l + jnp.sum(p, axis=0, keepdims=True)
            acc = alpha * acc + _dot(vt_ref[c, i * v_dim:(i + 1) * v_dim, :], p.astype(BF16))
            out.append((m_new, l, acc))
        return tuple(out)

    init = (jnp.full((1, tq), NEG_BIG, F32), jnp.zeros((1, tq), F32), jnp.zeros((v_dim, tq), F32))
    for g in range(0, n_heads, heads_per_step):
        heads = tuple(range(g, g + heads_per_step))
        carry = lax.fori_loop(0, n_full, functools.partial(step, heads=heads, masked=False), (init,) * heads_per_step)
        carry = lax.fori_loop(n_full, n_kv, functools.partial(step, heads=heads, masked=True), carry)
        for i, (_, l, acc) in zip(heads, carry):
            oh_ref[:, i * v_dim:(i + 1) * v_dim] = (acc / l).T.astype(BF16)
    o_ref[...] = x_ref[...] + _mod(mod_ref, MIXER_GATE) * _dot(oh_ref[...], wo_ref[...])


def _attn_prompt(q, k, vt, x, mod, w_o, layer, j, n_seq, n_heads, scale):
    rows, d = x.shape
    seq = rows // n_seq
    _, n_kv, hv, tk = vt.shape
    assert n_kv * tk == seq
    tq = _tile(seq, 512)
    r3 = lambda a: a.reshape(n_seq, seq, a.shape[-1])
    q_spec = lambda w: pl.BlockSpec((None, tq, w), lambda b, i: (b, i, 0))
    out = pl.pallas_call(
        functools.partial(_attn_kernel, n_heads=n_heads, heads_per_step=_tile(n_heads, 4),
                          v_dim=hv // n_heads, scale=scale),
        grid=(n_seq, seq // tq),
        in_specs=[
            q_spec(q.shape[-1]),
            pl.BlockSpec((None, seq, k.shape[-1]), lambda b, i: (b, 0, 0)),
            pl.BlockSpec((None, n_kv, hv, tk), lambda b, i: (b, 0, 0, 0)),
            q_spec(d),
            pl.BlockSpec((None, None, 1, mod.shape[-1]), lambda b, i: (layer, b, 0, 0)),
            _resident((None, hv, d), lambda b, i: (j, 0, 0)),
        ],
        out_specs=q_spec(d),
        out_shape=jax.ShapeDtypeStruct((n_seq, seq, d), F32),
        scratch_shapes=[pltpu.VMEM((tq, hv), BF16)],
        compiler_params=_params("arbitrary", "arbitrary"),
    )(r3(q), r3(k), vt, r3(x), mod, w_o)
    return out.reshape(rows, d)


def _decode_kernel(pt_ref, ql_ref, qr_ref, cn_ref, kn_ref, lat_hbm, krt_hbm, o_ref, lat_buf, krt_buf, sem,
                   *, layer, n_chunks, pages_per_chunk, page, n_heads, n_new, n_split, scale):
    b = pl.program_id(0)
    n_b = pl.num_programs(0)
    n_rows = ql_ref.shape[0]

    def copies(bb, c, slot):
        out = []
        for k in range(pages_per_chunk):
            pg = pt_ref[bb, c * pages_per_chunk + k]
            dst = pl.ds(k * page, page)
            out.append(pltpu.make_async_copy(lat_hbm.at[layer, pg], lat_buf.at[slot, dst], sem.at[0, slot]))
            out.append(pltpu.make_async_copy(krt_hbm.at[layer, pg], krt_buf.at[slot, k], sem.at[1, slot]))
        return out

    def start(bb, c, slot):
        for n, cp in enumerate(copies(bb, c, slot)):
            cp.start(priority=(n // 2) % 2)

    @pl.when(b == 0)
    def _():
        start(0, 0, 0)

    ql = ql_ref[...]
    qr = qr_ref[...]

    def online(carry, s, values):
        m, l, acc = carry
        m_new = jnp.maximum(m, jnp.max(s, axis=-1, keepdims=True))
        alpha = jnp.exp(m - m_new)
        p = jnp.exp(s - m_new)
        l = alpha * l + jnp.sum(p, axis=-1, keepdims=True)
        acc = alpha * acc + _dot(p.astype(BF16), values)
        return m_new, l, acc

    def step(c, carry):
        slot = (b * n_chunks + c) % 2
        last = c + 1 == n_chunks
        nb = jnp.where(last, b + 1, b)
        nc = jnp.where(last, 0, c + 1)

        @pl.when(nb < n_b)
        def _():
            start(nb, nc, 1 - slot)

        for cp in copies(b, c, slot):
            cp.wait()
        lats, scores = [], []
        for k in range(n_split):
            keys = slice(k * sub, (k + 1) * sub)
            lats.append(lat_buf[slot, keys, :].astype(BF16))
            pages = range(k * sub // page, (k + 1) * sub // page)
            krt = jnp.concatenate([krt_buf[slot, pg].astype(BF16) for pg in pages], axis=1)
            scores.append((_dot_nt(ql, lats[k]) + _dot(qr, krt)) * scale)
        return tuple(online(state, s, lat) for state, s, lat in zip(carry, scores, lats))

    sub = pages_per_chunk * page // n_split
    init = (jnp.full((n_rows, 1), NEG_BIG, F32), jnp.zeros((n_rows, 1), F32),
            jnp.zeros((n_rows, lat_buf.shape[-1]), F32))
    states = lax.fori_loop(0, n_chunks, step, (init,) * n_split)
    m = functools.reduce(jnp.maximum, [st[0] for st in states])
    weights = [jnp.exp(st[0] - m) for st in states]
    carry = (m, sum(w * st[1] for w, st in zip(weights, states)), sum(w * st[2] for w, st in zip(weights, states)))

    cn = cn_ref[...]
    s_new = (_dot_nt(ql, cn) + _dot_nt(qr, kn_ref[...])) * scale
    t = lax.broadcasted_iota(jnp.int32, (n_rows, 1), 0) // n_heads
    u = lax.broadcasted_iota(jnp.int32, (1, cn.shape[0]), 1)
    s_new = jnp.where((u <= t) & (u < n_new), s_new, -jnp.inf)
    _, l, acc = online(carry, s_new, cn)
    o_ref[...] = acc / l


def _decode_attn(page_table, q_lat, q_rope, ckv_new, kr_new, cache_lat, cache_krt, layer, n_heads, scale):
    n_b, n_rows, kv_lora = q_lat.shape
    rope = q_rope.shape[-1]
    n_new = n_rows // n_heads
    page = cache_lat.shape[2]
    n_pages = page_table.shape[1]
    pages_per_chunk = _tile(n_pages, 32)
    n_split = _tile(pages_per_chunk, 4)
    n_chunks = n_pages // pages_per_chunk
    key_pad = LANES
    assert n_new <= key_pad
    pad_keys = lambda a: jnp.pad(a.astype(BF16), ((0, 0), (0, key_pad - n_new), (0, 0)))
    per_b = lambda r, w: pl.BlockSpec((None, r, w), lambda b, pt: (b, 0, 0))
    return pl.pallas_call(
        functools.partial(_decode_kernel, layer=layer, n_chunks=n_chunks, pages_per_chunk=pages_per_chunk,
                          page=page, n_heads=n_heads, n_new=n_new, n_split=n_split,
                          scale=scale),
        grid_spec=pltpu.PrefetchScalarGridSpec(
            num_scalar_prefetch=1,
            grid=(n_b,),
            in_specs=[
                per_b(n_rows, kv_lora), per_b(n_rows, rope), per_b(key_pad, kv_lora), per_b(key_pad, rope),
                pl.BlockSpec(memory_space=pl.ANY), pl.BlockSpec(memory_space=pl.ANY),
            ],
            out_specs=per_b(n_rows, kv_lora),
            scratch_shapes=[
                pltpu.VMEM((2, pages_per_chunk * page, kv_lora), F32),
                pltpu.VMEM((2, pages_per_chunk, rope, page), F32),
                pltpu.SemaphoreType.DMA((2, 2)),
            ],
        ),
        out_shape=jax.ShapeDtypeStruct((n_b, n_rows, kv_lora), F32),
        compiler_params=_params("arbitrary"),
    )(page_table, q_lat, q_rope, pad_keys(ckv_new), pad_keys(kr_new), cache_lat, cache_krt)


def _decode_out_kernel(ol_ref, x_ref, mod_ref, wuv_ref, wo_ref, o_ref, oh_ref, *, n_heads, kv_lora, v_dim):
    for i in range(n_heads):
        o_i = ol_ref[:, i * kv_lora:(i + 1) * kv_lora].astype(BF16)
        oh_ref[:, i * v_dim:(i + 1) * v_dim] = _dot(o_i, wuv_ref[i]).astype(BF16)
    o_ref[...] = x_ref[...] + _mod(mod_ref, MIXER_GATE) * _dot(oh_ref[...], wo_ref[...])


def _decode_out(o_lat, x, mod, w_uv, w_o, layer, j):
    rows, d = x.shape
    kv_lora, n_heads, v_dim = w_uv.shape
    w_uv_h = jnp.transpose(w_uv, (1, 0, 2)).astype(BF16)
    tm = _tile(rows, 512)
    row = lambda w: pl.BlockSpec((tm, w), lambda i: (i, 0))
    return pl.pallas_call(
        functools.partial(_decode_out_kernel, n_heads=n_heads, kv_lora=kv_lora, v_dim=v_dim),
        grid=(rows // tm,),
        in_specs=[
            row(n_heads * kv_lora), row(d), _mod_spec(mod, layer, rows, tm),
            _resident(w_uv_h.shape, lambda i: (0, 0, 0)),
            _resident((None, n_heads * v_dim, d), lambda i: (j, 0, 0)),
        ],
        out_specs=row(d),
        out_shape=jax.ShapeDtypeStruct((rows, d), F32),
        scratch_shapes=[pltpu.VMEM((tm, n_heads * v_dim), BF16)],
        compiler_params=_params("arbitrary"),
    )(o_lat, x, mod, w_uv_h, w_o)


def kernel(x_prompt, x_sample, c_prompt, c_sample, state_conv, cache_kv_latent, cache_k_rope, page_table, w_ada, b_ada, g_norm, w_ffn_gu, w_ffn_down, w_conv_in, w_conv_k, w_conv_out, w_mla_in, g_q_norm, g_kv_norm, w_uq, w_uk, w_uv, w_mla_out, g_final):
    n_p, seq_p, d = x_prompt.shape
    n_s, seq_s, _ = x_sample.shape
    depth = w_ada.shape[0]
    past = page_table.shape[1] * cache_kv_latent.shape[2]
    n_heads, qk_dim = w_uq.shape[2], w_uq.shape[3]
    v_dim = w_uv.shape[3]
    kv_lora = g_kv_norm.shape[1]
    scale = qk_dim ** -0.5

    xp = x_prompt.reshape(n_p * seq_p, d)
    xs = x_sample.reshape(n_s * seq_s, d)
    mp, ms = _adaln(c_prompt, jnp.repeat(c_sample, seq_s, axis=0), w_ada, b_ada)
    mp = mp.reshape(depth, n_p, 1, N_MOD * d)
    g_norm4 = g_norm.reshape(depth, N_SUB, 1, d)
    w_gu = w_ffn_gu.astype(BF16)
    w_down = w_ffn_down.astype(BF16)
    w_cin = w_conv_in.astype(BF16)
    w_cout = w_conv_out.astype(BF16)
    w_mo = w_mla_out.astype(BF16)
    cache_krt = jnp.swapaxes(cache_k_rope, 2, 3)

    conv_p, conv_s, lat_p, kr_p, lat_s, kr_s = [], [], [], [], [], []
    for i in range(depth):
        last = i == depth - 1
        xp = _ffn(xp, mp, g_norm4, w_gu, w_down, i, 0, 0)
        xs = _ffn(xs, ms, g_norm4, w_gu, w_down, i, 0, 0)
        j = i // N_MIXERS
        if i % N_MIXERS == 0:
            xp, st_p = _conv_prompt(xp, mp, g_norm4, w_cin, w_conv_k, w_cout, i, j, n_p)
            xs, st_s = _conv_sample(xs, ms, g_norm4, w_cin, w_conv_k, w_cout, state_conv[j], i, j)
            conv_p.append(st_p)
            conv_s.append(st_s)
        else:
            proj = (w_mla_in[j], g_q_norm[j], g_kv_norm[j], w_uq[j], w_uk[j], w_uv[j])
            q, k, vt, ckv_p, k_p = _mla_project(
                xp, mp, g_norm4, i, *proj, jnp.arange(seq_p, dtype=jnp.int32), seq_p, absorb=False)
            xp = _attn_prompt(q, k, vt, xp, mp, w_mo, i, j, n_p, n_heads, scale)
            pos_s = jnp.tile(past + jnp.arange(seq_s, dtype=jnp.int32), n_s)
            ql, qrs, ckv_s, k_s = _mla_project(xs, ms, g_norm4, i, *proj, pos_s, n_s * seq_s, absorb=True)
            o_lat = _decode_attn(
                page_table, ql.reshape(n_s, seq_s * n_heads, kv_lora), qrs.reshape(n_s, seq_s * n_heads, -1),
                ckv_s.reshape(n_s, seq_s, -1), k_s.reshape(n_s, seq_s, -1),
                cache_kv_latent, cache_krt, j, n_heads, scale)
            xs = _decode_out(o_lat.reshape(n_s * seq_s, n_heads * kv_lora), xs, ms, w_uv[j], w_mo, i, j)
            lat_p.append(ckv_p.reshape(n_p, seq_p, -1))
            kr_p.append(k_p.reshape(n_p, seq_p, -1))
            lat_s.append(ckv_s.reshape(n_s, seq_s, -1))
            kr_s.append(k_s.reshape(n_s, seq_s, -1))
        xp = _ffn(xp, mp, g_norm4, w_gu, w_down, i, 1, 2, g_final if last else None)
        xs = _ffn(xs, ms, g_norm4, w_gu, w_down, i, 1, 2, g_final if last else None)
    return (xp.reshape(n_p, seq_p, d), xs.reshape(n_s, seq_s, d), jnp.stack(conv_p), jnp.stack(conv_s),
            jnp.stack(lat_p), jnp.stack(kr_p), jnp.stack(lat_s), jnp.stack(kr_s))
```

```python
import functools

import jax
import jax.numpy as jnp
from jax import lax
from jax.experimental import pallas as pl
from jax.experimental.pallas import tpu as pltpu

EPS = 1e-6
ROPE_THETA = 10000.0
N_SUB = 3
N_MOD = 3 * N_SUB
N_MIXERS = 2
MIXER_GATE = 3 * 1 + 2

BF16 = jnp.bfloat16
F32 = jnp.float32

LANES = 128
VMEM_LIMIT_BYTES = 56 * 1024 * 1024
NEG_BIG = -1e30
LOG2_E = 1.4426950408889634


def _params(*semantics):
    return pltpu.CompilerParams(dimension_semantics=semantics, vmem_limit_bytes=VMEM_LIMIT_BYTES)


def _dot(a, b):
    return jnp.dot(a, b, preferred_element_type=F32)


def _dot_nt(a, b):
    return lax.dot_general(a, b, (((1,), (1,)), ((), ())), preferred_element_type=F32)


def _rmsnorm(x, g):
    return x * lax.rsqrt(jnp.mean(x * x, axis=-1, keepdims=True) + EPS) * g


def _mod(mod_ref, k):
    d = mod_ref.shape[-1] // N_MOD
    return mod_ref[:, k * d:(k + 1) * d]


def _modulated(x, g, mod_ref, s):
    return _rmsnorm(x, g) * (1.0 + _mod(mod_ref, 3 * s + 1)) + _mod(mod_ref, 3 * s)


def _tile(n, want):
    t = min(n, want)
    while n % t:
        t -= 1
    return t


def _resident(block_shape, index_map):
    return pl.BlockSpec(block_shape, index_map, pipeline_mode=pl.Buffered(1))


def _rows_per_mod(mod, rows):
    return rows // mod.shape[1] if mod.ndim == 4 else 1


def _mod_spec(mod, layer, rows, tm):
    width = mod.shape[-1]
    if mod.ndim == 4:
        rows_per_seq = _rows_per_mod(mod, rows)
        assert rows_per_seq % tm == 0
        tiles_per_seq = rows_per_seq // tm
        return pl.BlockSpec((None, None, 1, width), lambda i, *_: (layer, i // tiles_per_seq, 0, 0))
    assert mod.shape[1] == rows
    return pl.BlockSpec((None, tm, width), lambda i, *_: (layer, i, 0))


def _adaln_kernel(cp_ref, cs_ref, w_ref, b_ref, op_ref, os_ref):
    w = w_ref[...].astype(BF16)
    for c_ref, o_ref in ((cp_ref, op_ref), (cs_ref, os_ref)):
        c = c_ref[...]
        o_ref[...] = _dot((c * jax.nn.sigmoid(c)).astype(BF16), w) + b_ref[...]


def _adaln(c_p, c_s, w_ada, b_ada):
    depth, d, n_out = w_ada.shape
    tn = _tile(n_out, 1024)
    c_spec = lambda c: _resident(c.shape, lambda l, j: (0, 0))
    o_spec = lambda c: pl.BlockSpec((None, c.shape[0], tn), lambda l, j: (l, 0, j))
    return pl.pallas_call(
        _adaln_kernel,
        grid=(depth, n_out // tn),
        in_specs=[
            c_spec(c_p), c_spec(c_s),
            pl.BlockSpec((None, d, tn), lambda l, j: (l, 0, j)),
            pl.BlockSpec((None, 1, tn), lambda l, j: (l, 0, j)),
        ],
        out_specs=[o_spec(c_p), o_spec(c_s)],
        out_shape=[jax.ShapeDtypeStruct((depth, c.shape[0], n_out), F32) for c in (c_p, c_s)],
        compiler_params=_params("arbitrary", "arbitrary"),
    )(c_p, c_s, w_ada, b_ada.reshape(depth, 1, n_out))


def _ffn_tile(x_ref, mod_ref, g_ref, wgu_ref, wd_ref, gf_ref, o_ref, h_ref, a_ref, s, tf, after_chunk=None):
    f = wd_ref.shape[0]
    h_ref[...] = _modulated(x_ref[...], g_ref[...], mod_ref, s).astype(BF16)
    for c in range(f // tf):
        gate = _dot(h_ref[...], wgu_ref[:, c * tf:(c + 1) * tf])
        up = _dot(h_ref[...], wgu_ref[:, f + c * tf:f + (c + 1) * tf])
        a_ref[:, c * tf:(c + 1) * tf] = (gate * jax.nn.sigmoid(gate) * up).astype(BF16)
        if after_chunk is not None:
            after_chunk(c)
    xn = x_ref[...] + 0.5 * _mod(mod_ref, 3 * s + 2) * _dot(a_ref[...], wd_ref[...])
    if gf_ref is not None:
        xn = _rmsnorm(xn, gf_ref[...])
    o_ref[...] = xn


def _ffn_kernel(*refs, s, tf, final_norm):
    if final_norm:
        x_ref, mod_ref, g_ref, wgu_ref, wd_ref, gf_ref, o_ref, h_ref, a_ref = refs
    else:
        x_ref, mod_ref, g_ref, wgu_ref, wd_ref, o_ref, h_ref, a_ref = refs
        gf_ref = None
    _ffn_tile(x_ref, mod_ref, g_ref, wgu_ref, wd_ref, gf_ref, o_ref, h_ref, a_ref, s, tf)


def _ffn_operands(x, mod, g_norm4, w_gu, w_down, layer, k, s, g_final):
    rows, d = x.shape
    f = w_down.shape[2]
    tm = _tile(rows if mod.ndim == 3 else _rows_per_mod(mod, rows), 512)
    in_specs = [
        pl.BlockSpec((tm, d), lambda i, *_: (i, 0)),
        _mod_spec(mod, layer, rows, tm),
        _resident((None, None, 1, d), lambda i, *_: (layer, s, 0, 0)),
        _resident((None, None, d, 2 * f), lambda i, *_: (layer, k, 0, 0)),
        _resident((None, None, f, d), lambda i, *_: (layer, k, 0, 0)),
    ]
    args = [x, mod, g_norm4, w_gu, w_down]
    if g_final is not None:
        in_specs.append(_resident((1, d), lambda i, *_: (0, 0)))
        args.append(g_final.reshape(1, d))
    return tm, _tile(f, 256), in_specs, args


def _ffn(x, mod, g_norm4, w_gu, w_down, layer, k, s, g_final=None):
    rows, d = x.shape
    f = w_down.shape[2]
    tm, tf, in_specs, args = _ffn_operands(x, mod, g_norm4, w_gu, w_down, layer, k, s, g_final)
    return pl.pallas_call(
        functools.partial(_ffn_kernel, s=s, tf=tf, final_norm=g_final is not None),
        grid=(rows // tm,),
        in_specs=in_specs,
        out_specs=pl.BlockSpec((tm, d), lambda i: (i, 0)),
        out_shape=jax.ShapeDtypeStruct((rows, d), F32),
        scratch_shapes=[pltpu.VMEM((tm, d), BF16), pltpu.VMEM((tm, f), BF16)],
        compiler_params=_params("arbitrary"),
    )(*args)


CONV_COLS = 256


def _conv_mixer(x_ref, mod_ref, g_ref, win_ref, wk_ref, wout_ref, o_ref, h_ref, bc_ref, shifted, keep):
    d = x_ref.shape[-1]
    h_ref[...] = _modulated(x_ref[...], g_ref[...], mod_ref, 1).astype(BF16)
    for c in range(0, d, CONV_COLS):
        cols = slice(c, c + CONV_COLS)
        proj = lambda part: _dot(h_ref[...], win_ref[:, part * d + c:part * d + c + CONV_COLS])
        b_gate = proj(0)
        u = proj(1) * proj(2)
        u1, u2 = shifted(u, cols)
        conv = wk_ref[0:1, cols] * u2 + wk_ref[1:2, cols] * u1 + wk_ref[2:3, cols] * u
        bc_ref[:, cols] = (b_gate * conv).astype(BF16)
        keep(u, cols)
    o_ref[...] = x_ref[...] + _mod(mod_ref, MIXER_GATE) * _dot(bc_ref[...], wout_ref[...])


def _conv_seq_kernel(x_ref, mod_ref, g_ref, win_ref, wk_ref, wout_ref, o_ref, st_ref, h_ref, bc_ref, carry_ref,
                     *, tiles_per_seq):
    tm = x_ref.shape[0]

    @pl.when(pl.program_id(0) % tiles_per_seq == 0)
    def _():
        carry_ref[...] = jnp.zeros_like(carry_ref)

    row = lax.broadcasted_iota(jnp.int32, (tm, 1), 0)

    def shifted(u, cols):
        prev0 = carry_ref[0:1, cols]
        prev1 = carry_ref[1:2, cols]
        u1 = jnp.where(row == 0, prev1, pltpu.roll(u, 1, axis=0))
        u2 = jnp.where(row == 0, prev0, jnp.where(row == 1, prev1, pltpu.roll(u, 2, axis=0)))
        return u1, u2

    def keep(u, cols):
        carry_ref[:, cols] = u[tm - 2:, :]
        st_ref[:, cols] = u[tm - 2:, :]

    _conv_mixer(x_ref, mod_ref, g_ref, win_ref, wk_ref, wout_ref, o_ref, h_ref, bc_ref, shifted, keep)


def _conv_step_kernel(x_ref, mod_ref, g_ref, win_ref, wk_ref, wout_ref, p1_ref, p2_ref, o_ref, u_ref, h_ref, bc_ref,
                      *, seq):
    tm = x_ref.shape[0]
    t = lax.broadcasted_iota(jnp.int32, (tm, 1), 0) % seq

    def shifted(u, cols):
        u1 = jnp.where(t >= 1, pltpu.roll(u, 1, axis=0), p1_ref[:, cols])
        u2 = jnp.where(t >= 2, pltpu.roll(u, 2, axis=0), p2_ref[:, cols])
        return u1, u2

    def keep(u, cols):
        u_ref[:, cols] = u

    _conv_mixer(x_ref, mod_ref, g_ref, win_ref, wk_ref, wout_ref, o_ref, h_ref, bc_ref, shifted, keep)


def _conv_scratch(tm, d):
    assert d % CONV_COLS == 0
    return [pltpu.VMEM((tm, d), BF16), pltpu.VMEM((tm, d), BF16)]


def _conv_weight_specs(d, n_taps, layer, j):
    return [
        _resident((None, None, 1, d), lambda i: (layer, 1, 0, 0)),
        _resident((None, d, 3 * d), lambda i: (j, 0, 0)),
        _resident((None, n_taps, d), lambda i: (j, 0, 0)),
        _resident((None, d, d), lambda i: (j, 0, 0)),
    ]


def _conv_prompt(x, mod, g_norm4, w_in, w_k, w_out, layer, j, n_seq):
    rows, d = x.shape
    seq = rows // n_seq
    assert w_k.shape[1] == 3 and seq >= 2
    tm = _tile(seq, 512)
    assert tm >= 2
    tiles_per_seq = seq // tm
    return pl.pallas_call(
        functools.partial(_conv_seq_kernel, tiles_per_seq=tiles_per_seq),
        grid=(rows // tm,),
        in_specs=[pl.BlockSpec((tm, d), lambda i: (i, 0)), _mod_spec(mod, layer, rows, tm)]
        + _conv_weight_specs(d, 3, layer, j),
        out_specs=[
            pl.BlockSpec((tm, d), lambda i: (i, 0)),
            pl.BlockSpec((None, 2, d), lambda i: (i // tiles_per_seq, 0, 0)),
        ],
        out_shape=[jax.ShapeDtypeStruct((rows, d), F32), jax.ShapeDtypeStruct((n_seq, 2, d), F32)],
        scratch_shapes=_conv_scratch(tm, d) + [pltpu.VMEM((2, d), F32)],
        compiler_params=_params("arbitrary"),
    )(x, mod, g_norm4, w_in, w_k, w_out)


def _conv_sample(x, mod, g_norm4, w_in, w_k, w_out, state, layer, j):
    rows, d = x.shape
    n_seq = state.shape[0]
    seq = rows // n_seq
    assert w_k.shape[1] == 3 and seq >= 2
    zeros = lambda n: jnp.zeros((n_seq, n, d), F32)
    prev1 = jnp.concatenate([state[:, 1:2], zeros(seq - 1)], axis=1).reshape(rows, d)
    prev2 = jnp.concatenate([state, zeros(seq - 2)], axis=1).reshape(rows, d)
    seqs_per_tile = _tile(n_seq, max(1, 512 // seq))
    tm = seqs_per_tile * seq
    row_spec = pl.BlockSpec((tm, d), lambda i: (i, 0))
    x_new, u = pl.pallas_call(
        functools.partial(_conv_step_kernel, seq=seq),
        grid=(rows // tm,),
        in_specs=[row_spec, _mod_spec(mod, layer, rows, tm)] + _conv_weight_specs(d, 3, layer, j) + [row_spec, row_spec],
        out_specs=[row_spec, row_spec],
        out_shape=[jax.ShapeDtypeStruct((rows, d), F32)] * 2,
        scratch_shapes=_conv_scratch(tm, d),
        compiler_params=_params("arbitrary"),
    )(x, mod, g_norm4, w_in, w_k, w_out, prev1, prev2)
    return x_new, u.reshape(n_seq, seq, d)[:, seq - 2:]


def _mla_proj_kernel(*refs, dims, absorb):
    q_lora, kv_lora, rope, n_heads, nope = dims
    if absorb:
        (x_ref, mod_ref, g_ref, win_ref, gq_ref, gkv_ref, wuq_ref, ck_ref, sk_ref, cq_ref, sq_ref, wukt_ref,
         ql_ref, qr_ref, ckv_ref, kr_ref) = refs
    else:
        (x_ref, mod_ref, g_ref, win_ref, gq_ref, gkv_ref, wuq_ref, ck_ref, sk_ref, wuk_ref, wuvt_ref,
         q_ref, k_ref, vt_ref, ckv_ref, kr_ref) = refs
    def rope_pair(a):
        return a * ck_ref[...] + pltpu.roll(a, LANES // 2, axis=1) * sk_ref[...]

    h = _modulated(x_ref[...], g_ref[...], mod_ref, 1).astype(BF16)
    dn = _dot(h, win_ref[...])
    cq = _rmsnorm(dn[:, :q_lora], gq_ref[...]).astype(BF16)
    ckv = _rmsnorm(dn[:, q_lora:q_lora + kv_lora], gkv_ref[...])
    ckv_ref[...] = ckv
    o = q_lora + kv_lora
    krp = rope_pair(dn[:, o:o + LANES])
    kr_ref[...] = krp[:, :rope]
    q = _dot(cq, wuq_ref[...])
    hn = n_heads * nope
    if absorb:
        hr = n_heads * rope
        qr_ref[...] = (q[:, hn:hn + hr] * cq_ref[...] + q[:, hn + hr:hn + 2 * hr] * sq_ref[...]).astype(BF16)
        for i in range(n_heads):
            qn_i = q[:, i * nope:(i + 1) * nope].astype(BF16)
            ql_ref[:, i * kv_lora:(i + 1) * kv_lora] = _dot(qn_i, wukt_ref[i]).astype(BF16)
    else:
        ckv_b = ckv.astype(BF16)
        kn = _dot(ckv_b, wuk_ref[...])
        krp_b = krp.astype(BF16)
        w = nope + LANES
        for i in range(n_heads):
            q_ref[:, i * w:i * w + nope] = q[:, i * nope:(i + 1) * nope].astype(BF16)
            q_ref[:, i * w + nope:(i + 1) * w] = rope_pair(q[:, hn + i * LANES:hn + (i + 1) * LANES]).astype(BF16)
            k_ref[:, i * w:i * w + nope] = kn[:, i * nope:(i + 1) * nope].astype(BF16)
            k_ref[:, i * w + nope:(i + 1) * w] = krp_b
        vt_ref[...] = _dot_nt(wuvt_ref[...], ckv_b).astype(BF16)


def _rope_tables(pos, rope, width, reps):
    half = rope // 2
    inv = ROPE_THETA ** (-jnp.arange(half, dtype=F32) * (2.0 / rope))
    ang = pos.astype(F32)[:, None] * inv[None, :]
    cos, sin = jnp.cos(ang), jnp.sin(ang)
    pad = jnp.zeros((pos.shape[0], width - rope), F32)
    c = jnp.concatenate([cos, cos, pad], axis=1)
    s = jnp.concatenate([-sin, sin, pad], axis=1)
    return jnp.tile(c, (1, reps)), jnp.tile(s, (1, reps))


def _swap_halves(w, rope):
    half = rope // 2
    return jnp.concatenate([w[..., half:], w[..., :half]], axis=-1)


def _mla_weights(w_in, w_uq, q_lora, kv_lora, rope, nope, absorb):
    assert 2 * rope == LANES
    n_heads = w_uq.shape[1]
    w_kr = w_in[:, q_lora + kv_lora:]
    w_in_ext = jnp.concatenate([w_in[:, :q_lora + kv_lora], w_kr, _swap_halves(w_kr, rope)], axis=1)
    w_qn = w_uq[:, :, :nope].reshape(q_lora, n_heads * nope)
    w_qr = w_uq[:, :, nope:]
    w_qr_swapped = _swap_halves(w_qr, rope)
    if absorb:
        w_rope = [w_qr.reshape(q_lora, n_heads * rope), w_qr_swapped.reshape(q_lora, n_heads * rope)]
    else:
        w_rope = [jnp.concatenate([w_qr, w_qr_swapped], axis=-1).reshape(q_lora, n_heads * LANES)]
    w_uq_ext = jnp.concatenate([w_qn] + w_rope, axis=1)
    return w_in_ext.astype(BF16), w_uq_ext.astype(BF16)


def _mla_project(x, mod, g_norm4, layer, w_in, g_q, g_kv, w_uq, w_uk, w_uv, pos_rows, rows_per_pos_table, absorb):
    rows, d = x.shape
    q_lora, kv_lora = g_q.shape[0], g_kv.shape[0]
    rope = w_in.shape[1] - q_lora - kv_lora
    n_heads, nope = w_uk.shape[1], w_uk.shape[2]
    v_dim = w_uv.shape[2]
    w_in_ext, w_uq_ext = _mla_weights(w_in, w_uq, q_lora, kv_lora, rope, nope, absorb)
    cos_k, sin_k = _rope_tables(pos_rows, rope, LANES, 1)
    period = rows_per_pos_table
    tm = _tile(period, 512)
    tiles_per_period = period // tm
    row = lambda w: pl.BlockSpec((tm, w), lambda i: (i, 0))
    tab = lambda w: pl.BlockSpec((tm, w), lambda i: (i % tiles_per_period, 0))
    full = lambda a: _resident(a.shape, lambda i: (0,) * a.ndim)
    hn = n_heads * nope
    in_specs = [
        row(d), _mod_spec(mod, layer, rows, tm),
        _resident((None, None, 1, d), lambda i: (layer, 1, 0, 0)),
        full(w_in_ext), _resident((1, q_lora), lambda i: (0, 0)), _resident((1, kv_lora), lambda i: (0, 0)),
        full(w_uq_ext), tab(LANES), tab(LANES),
    ]
    args = [x, mod, g_norm4, w_in_ext, g_q.reshape(1, q_lora), g_kv.reshape(1, kv_lora), w_uq_ext, cos_k, sin_k]
    if absorb:
        hr = n_heads * rope
        cos_q, sin_q = _rope_tables(pos_rows, rope, rope, n_heads)
        w_ukt = jnp.transpose(w_uk, (1, 2, 0)).astype(BF16)
        in_specs += [tab(hr), tab(hr), full(w_ukt)]
        args += [cos_q, sin_q, w_ukt]
        outs = [(n_heads * kv_lora, BF16), (hr, BF16)]
    else:
        qk_w = n_heads * (nope + LANES)
        hv = n_heads * v_dim
        w_uk2 = w_uk.reshape(kv_lora, hn).astype(BF16)
        w_uvt = w_uv.reshape(kv_lora, hv).T.astype(BF16)
        in_specs += [full(w_uk2), full(w_uvt)]
        args += [w_uk2, w_uvt]
        outs = [(qk_w, BF16), (qk_w, BF16)]
    out_specs = [row(w) for w, _ in outs]
    out_shape = [jax.ShapeDtypeStruct((rows, w), dt) for w, dt in outs]
    if not absorb:
        out_specs.append(pl.BlockSpec((None, None, hv, tm), lambda i: (i // tiles_per_period, i % tiles_per_period, 0, 0)))
        out_shape.append(jax.ShapeDtypeStruct((rows // period, tiles_per_period, hv, tm), BF16))
    out_specs += [row(kv_lora), row(rope)]
    out_shape += [jax.ShapeDtypeStruct((rows, kv_lora), F32), jax.ShapeDtypeStruct((rows, rope), F32)]
    return pl.pallas_call(
        functools.partial(_mla_proj_kernel, dims=(q_lora, kv_lora, rope, n_heads, nope), absorb=absorb),
        grid=(rows // tm,),
        in_specs=in_specs,
        out_specs=out_specs,
        out_shape=out_shape,
        compiler_params=_params("arbitrary"),
    )(*args)


def _attn_kernel(q_ref, k_ref, vt_ref, x_ref, mod_ref, wo_ref, o_ref, oh_ref,
                 *, n_heads, heads_per_step, v_dim, scale):
    tq = q_ref.shape[0]
    tk = vt_ref.shape[-1]
    w = q_ref.shape[1] // n_heads
    q0 = pl.program_id(1) * tq
    n_full = q0 // tk
    n_kv = (q0 + tq + tk - 1) // tk
    key0 = lax.broadcasted_iota(jnp.int32, (tk, 1), 0)
    query = q0 + lax.broadcasted_iota(jnp.int32, (1, tq), 1)

    log2e_scale = scale * LOG2_E

    def step(c, carry, heads, masked):
        off = pl.multiple_of(c * tk, tk)
        scores = []
        for i in heads:
            s = _dot_nt(k_ref[pl.ds(off, tk), i * w:(i + 1) * w], q_ref[:, i * w:(i + 1) * w])
            scores.append(jnp.where(query >= off + key0, s, -jnp.inf) if masked else s)
        out = []
        for i, s, (m, l, acc) in zip(heads, scores, carry):
            m_new = jnp.maximum(m, jnp.max(s, axis=0, keepdims=True) * log2e_scale)
            alpha = jnp.exp2(m - m_new)
            p = jnp.exp2(s * log2e_scale - m_new)
            l = alpha * l + jnp.sum(p, axis=0, keepdims=True)
            acc = alpha * acc + _dot(vt_ref[c, i * v_dim:(i + 1) * v_dim, :], p.astype(BF16))
            out.append((m_new, l, acc))
        return tuple(out)

    init = (jnp.full((1, tq), NEG_BIG, F32), jnp.zeros((1, tq), F32), jnp.zeros((v_dim, tq), F32))
    for g in range(0, n_heads, heads_per_step):
        heads = tuple(range(g, g + heads_per_step))
        carry = lax.fori_loop(0, n_full, functools.partial(step, heads=heads, masked=False), (init,) * heads_per_step)
        carry = lax.fori_loop(n_full, n_kv, functools.partial(step, heads=heads, masked=True), carry)
        for i, (_, l, acc) in zip(heads, carry):
            oh_ref[:, i * v_dim:(i + 1) * v_dim] = (acc / l).T.astype(BF16)
    o_ref[...] = x_ref[...] + _mod(mod_ref, MIXER_GATE) * _dot(oh_ref[...], wo_ref[...])


def _attn_prompt(q, k, vt, x, mod, w_o, layer, j, n_seq, n_heads, scale):
    rows, d = x.shape
    seq = rows // n_seq
    _, n_kv, hv, tk = vt.shape
    assert n_kv * tk == seq
    tq = _tile(seq, 512)
    r3 = lambda a: a.reshape(n_seq, seq, a.shape[-1])
    q_spec = lambda w: pl.BlockSpec((None, tq, w), lambda b, i: (b, i, 0))
    out = pl.pallas_call(
        functools.partial(_attn_kernel, n_heads=n_heads, heads_per_step=_tile(n_heads, 4),
                          v_dim=hv // n_heads, scale=scale),
        grid=(n_seq, seq // tq),
        in_specs=[
            q_spec(q.shape[-1]),
            pl.BlockSpec((None, seq, k.shape[-1]), lambda b, i: (b, 0, 0)),
            pl.BlockSpec((None, n_kv, hv, tk), lambda b, i: (b, 0, 0, 0)),
            q_spec(d),
            pl.BlockSpec((None, None, 1, mod.shape[-1]), lambda b, i: (layer, b, 0, 0)),
            _resident((None, hv, d), lambda b, i: (j, 0, 0)),
        ],
        out_specs=q_spec(d),
        out_shape=jax.ShapeDtypeStruct((n_seq, seq, d), F32),
        scratch_shapes=[pltpu.VMEM((tq, hv), BF16)],
        compiler_params=_params("arbitrary", "arbitrary"),
    )(r3(q), r3(k), vt, r3(x), mod, w_o)
    return out.reshape(rows, d)


def _page_copies(pt_ref, lat_hbm, krt_hbm, lat_buf, krt_buf, sem, layer, seq, chunk, slot):
    pages_per_chunk, _, page = krt_buf.shape[1:]
    out = []
    for k in range(pages_per_chunk):
        pg = pt_ref[seq, chunk * pages_per_chunk + k]
        out.append(pltpu.make_async_copy(
            lat_hbm.at[layer, pg], lat_buf.at[slot, pl.ds(k * page, page)], sem.at[0, slot]))
        out.append(pltpu.make_async_copy(krt_hbm.at[layer, pg], krt_buf.at[slot, k], sem.at[1, slot]))
    return out


def _start_pages(copies):
    for n, cp in enumerate(copies):
        cp.start(priority=(n // 2) % 2)


def _wait_pages(copies):
    for cp in copies:
        cp.wait()


def _softmax_update(state, s, values):
    m, l, acc = state
    m_new = jnp.maximum(m, jnp.max(s, axis=-1, keepdims=True))
    alpha = jnp.exp(m - m_new)
    p = jnp.exp(s - m_new)
    l = alpha * l + jnp.sum(p, axis=-1, keepdims=True)
    acc = alpha * acc + _dot(p.astype(BF16), values)
    return m_new, l, acc


def _decode_init(n_rows, kv_lora, n_split):
    init = (jnp.full((n_rows, 1), NEG_BIG, F32), jnp.zeros((n_rows, 1), F32), jnp.zeros((n_rows, kv_lora), F32))
    return (init,) * n_split


def _decode_chunk(states, ql, qr, lat_buf, krt_buf, slot, scale):
    page = krt_buf.shape[-1]
    sub = lat_buf.shape[1] // len(states)
    lats, scores = [], []
    for k in range(len(states)):
        lats.append(lat_buf[slot, k * sub:(k + 1) * sub, :].astype(BF16))
        pages = range(k * sub // page, (k + 1) * sub // page)
        krt = jnp.concatenate([krt_buf[slot, pg].astype(BF16) for pg in pages], axis=1)
        scores.append((_dot_nt(ql, lats[k]) + _dot(qr, krt)) * scale)
    return tuple(_softmax_update(state, s, lat) for state, s, lat in zip(states, scores, lats))


def _decode_finish(states, ql, qr, cn, kn, n_heads, n_new, scale):
    m = functools.reduce(jnp.maximum, [st[0] for st in states])
    weights = [jnp.exp(st[0] - m) for st in states]
    merged = (m, sum(w * st[1] for w, st in zip(weights, states)), sum(w * st[2] for w, st in zip(weights, states)))
    s_new = (_dot_nt(ql, cn) + _dot_nt(qr, kn)) * scale
    t = lax.broadcasted_iota(jnp.int32, (ql.shape[0], 1), 0) // n_heads
    u = lax.broadcasted_iota(jnp.int32, (1, cn.shape[0]), 1)
    s_new = jnp.where((u <= t) & (u < n_new), s_new, -jnp.inf)
    _, l, acc = _softmax_update(merged, s_new, cn)
    return acc / l


def _decode_kernel(pt_ref, ql_ref, qr_ref, cn_ref, kn_ref, lat_hbm, krt_hbm, o_ref, lat_buf, krt_buf, sem,
                   *, layer, n_chunks, n_heads, n_new, n_split, scale):
    b = pl.program_id(0)
    n_b = pl.num_programs(0)
    copies = functools.partial(_page_copies, pt_ref, lat_hbm, krt_hbm, lat_buf, krt_buf, sem, layer)

    @pl.when(b == 0)
    def _():
        _start_pages(copies(0, 0, 0))

    ql = ql_ref[...]
    qr = qr_ref[...]

    def step(c, states):
        slot = (b * n_chunks + c) % 2
        last = c + 1 == n_chunks
        nb = jnp.where(last, b + 1, b)
        nc = jnp.where(last, 0, c + 1)

        @pl.when(nb < n_b)
        def _():
            _start_pages(copies(nb, nc, 1 - slot))

        _wait_pages(copies(b, c, slot))
        return _decode_chunk(states, ql, qr, lat_buf, krt_buf, slot, scale)

    states = lax.fori_loop(0, n_chunks, step, _decode_init(ql.shape[0], lat_buf.shape[-1], n_split))
    o_ref[...] = _decode_finish(states, ql, qr, cn_ref[...], kn_ref[...], n_heads, n_new, scale)


def _decode_setup(dec):
    page_table, q_lat, q_rope, ckv_new, kr_new, cache_lat, cache_krt, cache_layer, n_heads, scale = dec
    _, n_rows, kv_lora = q_lat.shape
    rope = q_rope.shape[-1]
    n_new = n_rows // n_heads
    page = cache_lat.shape[2]
    n_pages = page_table.shape[1]
    pages_per_chunk = _tile(n_pages, 32)
    assert n_new <= LANES
    pad_keys = lambda a: jnp.pad(a.astype(BF16), ((0, 0), (0, LANES - n_new), (0, 0)))
    operands = [q_lat, q_rope, pad_keys(ckv_new), pad_keys(kr_new), cache_lat, cache_krt]
    block_dims = [(n_rows, kv_lora), (n_rows, rope), (LANES, kv_lora), (LANES, rope)]
    scratch = [
        pltpu.VMEM((2, pages_per_chunk * page, kv_lora), F32),
        pltpu.VMEM((2, pages_per_chunk, rope, page), F32),
        pltpu.SemaphoreType.DMA((2, 2)),
    ]
    static = dict(layer=cache_layer, n_chunks=n_pages // pages_per_chunk, n_heads=n_heads, n_new=n_new,
                  n_split=_tile(pages_per_chunk, 4), scale=scale)
    return page_table, operands, block_dims, scratch, static


def _decode_attn(dec):
    page_table, operands, block_dims, scratch, static = _decode_setup(dec)
    n_b, n_rows, kv_lora = operands[0].shape
    per_b = lambda dims: pl.BlockSpec((None,) + dims, lambda b, pt: (b, 0, 0))
    any_space = pl.BlockSpec(memory_space=pl.ANY)
    return pl.pallas_call(
        functools.partial(_decode_kernel, **static),
        grid_spec=pltpu.PrefetchScalarGridSpec(
            num_scalar_prefetch=1,
            grid=(n_b,),
            in_specs=[per_b(dims) for dims in block_dims] + [any_space, any_space],
            out_specs=per_b((n_rows, kv_lora)),
            scratch_shapes=scratch,
        ),
        out_shape=jax.ShapeDtypeStruct((n_b, n_rows, kv_lora), F32),
        compiler_params=_params("arbitrary"),
    )(page_table, *operands)


def _ffn_decode_kernel(pt_ref, *refs, s, tf, final_norm, seq0, layer, n_chunks, n_heads, n_new, n_split, scale):
    if final_norm:
        x_ref, mod_ref, g_ref, wgu_ref, wd_ref, gf_ref, *refs = refs
    else:
        x_ref, mod_ref, g_ref, wgu_ref, wd_ref, *refs = refs
        gf_ref = None
    ql_ref, qr_ref, cn_ref, kn_ref, lat_hbm, krt_hbm, o_ref, ol_ref, h_ref, a_ref, lat_buf, krt_buf, sem = refs
    i = pl.program_id(0)
    n = pl.num_programs(0)
    seqs = ql_ref.shape[0]
    first = seq0 + i * seqs
    copies = functools.partial(_page_copies, pt_ref, lat_hbm, krt_hbm, lat_buf, krt_buf, sem, layer)
    items = [(q, c) for q in range(seqs) for c in range(n_chunks)]
    assert len(items) % 2 == 0

    @pl.when(i == 0)
    def _():
        _start_pages(copies(first, 0, 0))

    states = {}

    def chunk_step(w):
        q, c = items[w]
        if w + 1 < len(items):
            _start_pages(copies(first + items[w + 1][0], items[w + 1][1], (w + 1) % 2))
        else:
            _start_pages(copies(jnp.where(i + 1 < n, first + seqs, seq0), 0, 0))
        _wait_pages(copies(first + q, c, w % 2))
        ql, qr = ql_ref[q], qr_ref[q]
        prev = _decode_init(ql.shape[0], lat_buf.shape[-1], n_split) if c == 0 else states[q]
        states[q] = _decode_chunk(prev, ql, qr, lat_buf, krt_buf, w % 2, scale)
        if c == n_chunks - 1:
            ol_ref[q] = _decode_finish(states[q], ql, qr, cn_ref[q], kn_ref[q], n_heads, n_new, scale)

    n_f = wd_ref.shape[0] // tf
    after = {}
    for w in range(len(items)):
        after.setdefault(min(n_f - 1, (2 * w + 1) * n_f // (2 * len(items))), []).append(w)

    def after_chunk(c):
        for w in after.get(c, ()):
            chunk_step(w)

    _ffn_tile(x_ref, mod_ref, g_ref, wgu_ref, wd_ref, gf_ref, o_ref, h_ref, a_ref, s, tf, after_chunk)

    @pl.when(i == n - 1)
    def _():
        _wait_pages(copies(seq0, 0, 0))


def _decode_fits_ffn_calls(dec, n_calls, x_rows, mod):
    page_table, q_lat = dec[0], dec[1]
    n_steps = x_rows // _tile(_rows_per_mod(mod, x_rows), 512)
    n_chunks = page_table.shape[1] // _tile(page_table.shape[1], 32)
    n_seq = q_lat.shape[0]
    return n_seq % (n_calls * n_steps) == 0 and (n_seq // (n_calls * n_steps) * n_chunks) % 2 == 0


def _ffn_with_decode(x, mod, g_norm4, w_gu, w_down, layer, k, s, g_final, dec, seq0, n_seq):
    rows, d = x.shape
    f = w_down.shape[2]
    tm, tf, in_specs, args = _ffn_operands(x, mod, g_norm4, w_gu, w_down, layer, k, s, g_final)
    page_table, operands, block_dims, scratch, static = _decode_setup(dec)
    n_steps = rows // tm
    seqs = n_seq // n_steps
    assert seqs * n_steps == n_seq and seq0 % seqs == 0
    per_step = lambda dims, blk0: pl.BlockSpec((seqs,) + dims, lambda i, pt: (blk0 + i, 0, 0))
    any_space = pl.BlockSpec(memory_space=pl.ANY)
    n_rows, kv_lora = block_dims[0]
    return pl.pallas_call(
        functools.partial(_ffn_decode_kernel, s=s, tf=tf, final_norm=g_final is not None, seq0=seq0, **static),
        grid_spec=pltpu.PrefetchScalarGridSpec(
            num_scalar_prefetch=1,
            grid=(n_steps,),
            in_specs=in_specs + [per_step(dims, seq0 // seqs) for dims in block_dims] + [any_space, any_space],
            out_specs=[pl.BlockSpec((tm, d), lambda i, pt: (i, 0)), per_step((n_rows, kv_lora), 0)],
            scratch_shapes=[pltpu.VMEM((tm, d), BF16), pltpu.VMEM((tm, f), BF16)] + scratch,
        ),
        out_shape=[jax.ShapeDtypeStruct((rows, d), F32), jax.ShapeDtypeStruct((n_seq, n_rows, kv_lora), F32)],
        compiler_params=_params("arbitrary"),
    )(page_table, *args, *operands)


def _decode_out_kernel(ol_ref, x_ref, mod_ref, wuv_ref, wo_ref, o_ref, oh_ref, *, n_heads, kv_lora, v_dim):
    for i in range(n_heads):
        o_i = ol_ref[:, i * kv_lora:(i + 1) * kv_lora].astype(BF16)
        oh_ref[:, i * v_dim:(i + 1) * v_dim] = _dot(o_i, wuv_ref[i]).astype(BF16)
    o_ref[...] = x_ref[...] + _mod(mod_ref, MIXER_GATE) * _dot(oh_ref[...], wo_ref[...])


def _decode_out(o_lat, x, mod, w_uv, w_o, layer, j):
    rows, d = x.shape
    kv_lora, n_heads, v_dim = w_uv.shape
    w_uv_h = jnp.transpose(w_uv, (1, 0, 2)).astype(BF16)
    tm = _tile(rows, 512)
    row = lambda w: pl.BlockSpec((tm, w), lambda i: (i, 0))
    return pl.pallas_call(
        functools.partial(_decode_out_kernel, n_heads=n_heads, kv_lora=kv_lora, v_dim=v_dim),
        grid=(rows // tm,),
        in_specs=[
            row(n_heads * kv_lora), row(d), _mod_spec(mod, layer, rows, tm),
            _resident(w_uv_h.shape, lambda i: (0, 0, 0)),
            _resident((None, n_heads * v_dim, d), lambda i: (j, 0, 0)),
        ],
        out_specs=row(d),
        out_shape=jax.ShapeDtypeStruct((rows, d), F32),
        scratch_shapes=[pltpu.VMEM((tm, n_heads * v_dim), BF16)],
        compiler_params=_params("arbitrary"),
    )(o_lat, x, mod, w_uv_h, w_o)


def kernel(x_prompt, x_sample, c_prompt, c_sample, state_conv, cache_kv_latent, cache_k_rope, page_table, w_ada, b_ada, g_norm, w_ffn_gu, w_ffn_down, w_conv_in, w_conv_k, w_conv_out, w_mla_in, g_q_norm, g_kv_norm, w_uq, w_uk, w_uv, w_mla_out, g_final):
    n_p, seq_p, d = x_prompt.shape
    n_s, seq_s, _ = x_sample.shape
    depth = w_ada.shape[0]
    past = page_table.shape[1] * cache_kv_latent.shape[2]
    n_heads, qk_dim = w_uq.shape[2], w_uq.shape[3]
    v_dim = w_uv.shape[3]
    kv_lora = g_kv_norm.shape[1]
    scale = qk_dim ** -0.5

    xp = x_prompt.reshape(n_p * seq_p, d)
    xs = x_sample.reshape(n_s * seq_s, d)
    mp, ms = _adaln(c_prompt, jnp.repeat(c_sample, seq_s, axis=0), w_ada, b_ada)
    mp = mp.reshape(depth, n_p, 1, N_MOD * d)
    g_norm4 = g_norm.reshape(depth, N_SUB, 1, d)
    w_gu = w_ffn_gu.astype(BF16)
    w_down = w_ffn_down.astype(BF16)
    w_cin = w_conv_in.astype(BF16)
    w_cout = w_conv_out.astype(BF16)
    w_mo = w_mla_out.astype(BF16)
    cache_krt = jnp.swapaxes(cache_k_rope, 2, 3)

    conv_p, conv_s, lat_p, kr_p, lat_s, kr_s = [], [], [], [], [], []
    for i in range(depth):
        gf = g_final if i == depth - 1 else None
        ffn_a = lambda x, m: _ffn(x, m, g_norm4, w_gu, w_down, i, 0, 0)
        ffn_b = lambda x, m: _ffn(x, m, g_norm4, w_gu, w_down, i, 1, 2, gf)
        j = i // N_MIXERS
        if i % N_MIXERS == 0:
            xp, xs = ffn_a(xp, mp), ffn_a(xs, ms)
            xp, st_p = _conv_prompt(xp, mp, g_norm4, w_cin, w_conv_k, w_cout, i, j, n_p)
            xs, st_s = _conv_sample(xs, ms, g_norm4, w_cin, w_conv_k, w_cout, state_conv[j], i, j)
            conv_p.append(st_p)
            conv_s.append(st_s)
            xp, xs = ffn_b(xp, mp), ffn_b(xs, ms)
            continue
        proj = (w_mla_in[j], g_q_norm[j], g_kv_norm[j], w_uq[j], w_uk[j], w_uv[j])
        xs = ffn_a(xs, ms)
        pos_s = jnp.tile(past + jnp.arange(seq_s, dtype=jnp.int32), n_s)
        ql, qrs, ckv_s, k_s = _mla_project(xs, ms, g_norm4, i, *proj, pos_s, n_s * seq_s, absorb=True)
        dec = (page_table, ql.reshape(n_s, seq_s * n_heads, kv_lora), qrs.reshape(n_s, seq_s * n_heads, -1),
               ckv_s.reshape(n_s, seq_s, -1), k_s.reshape(n_s, seq_s, -1), cache_kv_latent, cache_krt, j, n_heads, scale)
        fused = _decode_fits_ffn_calls(dec, 2, xp.shape[0], mp)
        half = n_s // 2
        if fused:
            xp, o_lat_a = _ffn_with_decode(xp, mp, g_norm4, w_gu, w_down, i, 0, 0, None, dec, 0, half)
        else:
            xp = ffn_a(xp, mp)
        q, k, vt, ckv_p, k_p = _mla_project(
            xp, mp, g_norm4, i, *proj, jnp.arange(seq_p, dtype=jnp.int32), seq_p, absorb=False)
        xp = _attn_prompt(q, k, vt, xp, mp, w_mo, i, j, n_p, n_heads, scale)
        if fused:
            xp, o_lat_b = _ffn_with_decode(xp, mp, g_norm4, w_gu, w_down, i, 1, 2, gf, dec, half, half)
            o_lat = jnp.concatenate([o_lat_a, o_lat_b], axis=0)
        else:
            xp = ffn_b(xp, mp)
            o_lat = _decode_attn(dec)
        xs = _decode_out(o_lat.reshape(n_s * seq_s, n_heads * kv_lora), xs, ms, w_uv[j], w_mo, i, j)
        xs = ffn_b(xs, ms)
        lat_p.append(ckv_p.reshape(n_p, seq_p, -1))
        kr_p.append(k_p.reshape(n_p, seq_p, -1))
        lat_s.append(ckv_s.reshape(n_s, seq_s, -1))
        kr_s.append(k_s.reshape(n_s, seq_s, -1))
    return (xp.reshape(n_p, seq_p, d), xs.reshape(n_s, seq_s, d), jnp.stack(conv_p), jnp.stack(conv_s),
            jnp.stack(lat_p), jnp.stack(kr_p), jnp.stack(lat_s), jnp.stack(kr_s))
```

```python
import functools

import jax
import jax.numpy as jnp
from jax import lax
from jax.experimental import pallas as pl
from jax.experimental.pallas import tpu as pltpu

EPS = 1e-6
ROPE_THETA = 10000.0
N_SUB = 3
N_MOD = 3 * N_SUB
N_MIXERS = 2
MIXER_GATE = 3 * 1 + 2

BF16 = jnp.bfloat16
F32 = jnp.float32

LANES = 128
VMEM_LIMIT_BYTES = 56 * 1024 * 1024
NEG_BIG = -1e30
LOG2_E = 1.4426950408889634


def _params(*semantics):
    return pltpu.CompilerParams(dimension_semantics=semantics, vmem_limit_bytes=VMEM_LIMIT_BYTES)


def _dot(a, b):
    return jnp.dot(a, b, preferred_element_type=F32)


def _dot_nt(a, b):
    return lax.dot_general(a, b, (((1,), (1,)), ((), ())), preferred_element_type=F32)


def _rmsnorm(x, g):
    return x * lax.rsqrt(jnp.mean(x * x, axis=-1, keepdims=True) + EPS) * g


def _mod(mod_ref, k):
    d = mod_ref.shape[-1] // N_MOD
    return mod_ref[:, k * d:(k + 1) * d]


def _modulated(x, g, mod_ref, s):
    return _rmsnorm(x, g) * (1.0 + _mod(mod_ref, 3 * s + 1)) + _mod(mod_ref, 3 * s)


def _tile(n, want):
    t = min(n, want)
    while n % t:
        t -= 1
    return t


def _resident(block_shape, index_map):
    return pl.BlockSpec(block_shape, index_map, pipeline_mode=pl.Buffered(1))


def _rows_per_mod(mod, rows):
    return rows // mod.shape[1] if mod.ndim == 4 else 1


def _mod_spec(mod, layer, rows, tm):
    width = mod.shape[-1]
    if mod.ndim == 4:
        rows_per_seq = _rows_per_mod(mod, rows)
        assert rows_per_seq % tm == 0
        tiles_per_seq = rows_per_seq // tm
        return pl.BlockSpec((None, None, 1, width), lambda i, *_: (layer, i // tiles_per_seq, 0, 0))
    assert mod.shape[1] == rows
    return pl.BlockSpec((None, tm, width), lambda i, *_: (layer, i, 0))


def _adaln_kernel(cp_ref, cs_ref, w_ref, b_ref, op_ref, os_ref):
    w = w_ref[...].astype(BF16)
    for c_ref, o_ref in ((cp_ref, op_ref), (cs_ref, os_ref)):
        c = c_ref[...]
        o_ref[...] = _dot((c * jax.nn.sigmoid(c)).astype(BF16), w) + b_ref[...]


def _adaln(c_p, c_s, w_ada, b_ada):
    depth, d, n_out = w_ada.shape
    tn = _tile(n_out, 1024)
    c_spec = lambda c: _resident(c.shape, lambda l, j: (0, 0))
    o_spec = lambda c: pl.BlockSpec((None, c.shape[0], tn), lambda l, j: (l, 0, j))
    return pl.pallas_call(
        _adaln_kernel,
        grid=(depth, n_out // tn),
        in_specs=[
            c_spec(c_p), c_spec(c_s),
            pl.BlockSpec((None, d, tn), lambda l, j: (l, 0, j)),
            pl.BlockSpec((None, 1, tn), lambda l, j: (l, 0, j)),
        ],
        out_specs=[o_spec(c_p), o_spec(c_s)],
        out_shape=[jax.ShapeDtypeStruct((depth, c.shape[0], n_out), F32) for c in (c_p, c_s)],
        compiler_params=_params("arbitrary", "arbitrary"),
    )(c_p, c_s, w_ada, b_ada.reshape(depth, 1, n_out))


def _ffn_tile(x_ref, mod_ref, g_ref, wgu_ref, wd_ref, gf_ref, o_ref, h_ref, a_ref, s, tf, after_dot=None):
    f = wd_ref.shape[0]
    after_dot = after_dot or (lambda n: None)
    h_ref[...] = _modulated(x_ref[...], g_ref[...], mod_ref, s).astype(BF16)
    for c in range(f // tf):
        gate = _dot(h_ref[...], wgu_ref[:, c * tf:(c + 1) * tf])
        after_dot(2 * c)
        up = _dot(h_ref[...], wgu_ref[:, f + c * tf:f + (c + 1) * tf])
        a_ref[:, c * tf:(c + 1) * tf] = (gate * jax.nn.sigmoid(gate) * up).astype(BF16)
        after_dot(2 * c + 1)
    xn = x_ref[...] + 0.5 * _mod(mod_ref, 3 * s + 2) * _dot(a_ref[...], wd_ref[...])
    if gf_ref is not None:
        xn = _rmsnorm(xn, gf_ref[...])
    o_ref[...] = xn


def _ffn_kernel(*refs, s, tf, final_norm):
    if final_norm:
        x_ref, mod_ref, g_ref, wgu_ref, wd_ref, gf_ref, o_ref, h_ref, a_ref = refs
    else:
        x_ref, mod_ref, g_ref, wgu_ref, wd_ref, o_ref, h_ref, a_ref = refs
        gf_ref = None
    _ffn_tile(x_ref, mod_ref, g_ref, wgu_ref, wd_ref, gf_ref, o_ref, h_ref, a_ref, s, tf)


def _ffn_operands(x, mod, g_norm4, w_gu, w_down, layer, k, s, g_final):
    rows, d = x.shape
    f = w_down.shape[2]
    tm = _tile(rows if mod.ndim == 3 else _rows_per_mod(mod, rows), 512)
    in_specs = [
        pl.BlockSpec((tm, d), lambda i, *_: (i, 0)),
        _mod_spec(mod, layer, rows, tm),
        _resident((None, None, 1, d), lambda i, *_: (layer, s, 0, 0)),
        _resident((None, None, d, 2 * f), lambda i, *_: (layer, k, 0, 0)),
        _resident((None, None, f, d), lambda i, *_: (layer, k, 0, 0)),
    ]
    args = [x, mod, g_norm4, w_gu, w_down]
    if g_final is not None:
        in_specs.append(_resident((1, d), lambda i, *_: (0, 0)))
        args.append(g_final.reshape(1, d))
    return tm, _tile(f, 256), in_specs, args


def _ffn(x, mod, g_norm4, w_gu, w_down, layer, k, s, g_final=None):
    rows, d = x.shape
    f = w_down.shape[2]
    tm, tf, in_specs, args = _ffn_operands(x, mod, g_norm4, w_gu, w_down, layer, k, s, g_final)
    return pl.pallas_call(
        functools.partial(_ffn_kernel, s=s, tf=tf, final_norm=g_final is not None),
        grid=(rows // tm,),
        in_specs=in_specs,
        out_specs=pl.BlockSpec((tm, d), lambda i: (i, 0)),
        out_shape=jax.ShapeDtypeStruct((rows, d), F32),
        scratch_shapes=[pltpu.VMEM((tm, d), BF16), pltpu.VMEM((tm, f), BF16)],
        compiler_params=_params("arbitrary"),
    )(*args)


CONV_COLS = 256


def _conv_mixer(x_ref, mod_ref, g_ref, win_ref, wk_ref, wout_ref, o_ref, h_ref, bc_ref, shifted, keep):
    d = x_ref.shape[-1]
    h_ref[...] = _modulated(x_ref[...], g_ref[...], mod_ref, 1).astype(BF16)
    for c in range(0, d, CONV_COLS):
        cols = slice(c, c + CONV_COLS)
        proj = lambda part: _dot(h_ref[...], win_ref[:, part * d + c:part * d + c + CONV_COLS])
        b_gate = proj(0)
        u = proj(1) * proj(2)
        u1, u2 = shifted(u, cols)
        conv = wk_ref[0:1, cols] * u2 + wk_ref[1:2, cols] * u1 + wk_ref[2:3, cols] * u
        bc_ref[:, cols] = (b_gate * conv).astype(BF16)
        keep(u, cols)
    o_ref[...] = x_ref[...] + _mod(mod_ref, MIXER_GATE) * _dot(bc_ref[...], wout_ref[...])


def _conv_seq_kernel(x_ref, mod_ref, g_ref, win_ref, wk_ref, wout_ref, o_ref, st_ref, h_ref, bc_ref, carry_ref,
                     *, tiles_per_seq):
    tm = x_ref.shape[0]

    @pl.when(pl.program_id(0) % tiles_per_seq == 0)
    def _():
        carry_ref[...] = jnp.zeros_like(carry_ref)

    row = lax.broadcasted_iota(jnp.int32, (tm, 1), 0)

    def shifted(u, cols):
        prev0 = carry_ref[0:1, cols]
        prev1 = carry_ref[1:2, cols]
        u1 = jnp.where(row == 0, prev1, pltpu.roll(u, 1, axis=0))
        u2 = jnp.where(row == 0, prev0, jnp.where(row == 1, prev1, pltpu.roll(u, 2, axis=0)))
        return u1, u2

    def keep(u, cols):
        carry_ref[:, cols] = u[tm - 2:, :]
        st_ref[:, cols] = u[tm - 2:, :]

    _conv_mixer(x_ref, mod_ref, g_ref, win_ref, wk_ref, wout_ref, o_ref, h_ref, bc_ref, shifted, keep)


def _conv_step_kernel(x_ref, mod_ref, g_ref, win_ref, wk_ref, wout_ref, p1_ref, p2_ref, o_ref, u_ref, h_ref, bc_ref,
                      *, seq):
    tm = x_ref.shape[0]
    t = lax.broadcasted_iota(jnp.int32, (tm, 1), 0) % seq

    def shifted(u, cols):
        u1 = jnp.where(t >= 1, pltpu.roll(u, 1, axis=0), p1_ref[:, cols])
        u2 = jnp.where(t >= 2, pltpu.roll(u, 2, axis=0), p2_ref[:, cols])
        return u1, u2

    def keep(u, cols):
        u_ref[:, cols] = u

    _conv_mixer(x_ref, mod_ref, g_ref, win_ref, wk_ref, wout_ref, o_ref, h_ref, bc_ref, shifted, keep)


def _conv_scratch(tm, d):
    assert d % CONV_COLS == 0
    return [pltpu.VMEM((tm, d), BF16), pltpu.VMEM((tm, d), BF16)]


def _conv_weight_specs(d, n_taps, layer, j):
    return [
        _resident((None, None, 1, d), lambda i: (layer, 1, 0, 0)),
        _resident((None, d, 3 * d), lambda i: (j, 0, 0)),
        _resident((None, n_taps, d), lambda i: (j, 0, 0)),
        _resident((None, d, d), lambda i: (j, 0, 0)),
    ]


def _conv_prompt(x, mod, g_norm4, w_in, w_k, w_out, layer, j, n_seq):
    rows, d = x.shape
    seq = rows // n_seq
    assert w_k.shape[1] == 3 and seq >= 2
    tm = _tile(seq, 512)
    assert tm >= 2
    tiles_per_seq = seq // tm
    return pl.pallas_call(
        functools.partial(_conv_seq_kernel, tiles_per_seq=tiles_per_seq),
        grid=(rows // tm,),
        in_specs=[pl.BlockSpec((tm, d), lambda i: (i, 0)), _mod_spec(mod, layer, rows, tm)]
        + _conv_weight_specs(d, 3, layer, j),
        out_specs=[
            pl.BlockSpec((tm, d), lambda i: (i, 0)),
            pl.BlockSpec((None, 2, d), lambda i: (i // tiles_per_seq, 0, 0)),
        ],
        out_shape=[jax.ShapeDtypeStruct((rows, d), F32), jax.ShapeDtypeStruct((n_seq, 2, d), F32)],
        scratch_shapes=_conv_scratch(tm, d) + [pltpu.VMEM((2, d), F32)],
        compiler_params=_params("arbitrary"),
    )(x, mod, g_norm4, w_in, w_k, w_out)


def _conv_sample(x, mod, g_norm4, w_in, w_k, w_out, state, layer, j):
    rows, d = x.shape
    n_seq = state.shape[0]
    seq = rows // n_seq
    assert w_k.shape[1] == 3 and seq >= 2
    zeros = lambda n: jnp.zeros((n_seq, n, d), F32)
    prev1 = jnp.concatenate([state[:, 1:2], zeros(seq - 1)], axis=1).reshape(rows, d)
    prev2 = jnp.concatenate([state, zeros(seq - 2)], axis=1).reshape(rows, d)
    seqs_per_tile = _tile(n_seq, max(1, 512 // seq))
    tm = seqs_per_tile * seq
    row_spec = pl.BlockSpec((tm, d), lambda i: (i, 0))
    x_new, u = pl.pallas_call(
        functools.partial(_conv_step_kernel, seq=seq),
        grid=(rows // tm,),
        in_specs=[row_spec, _mod_spec(mod, layer, rows, tm)] + _conv_weight_specs(d, 3, layer, j) + [row_spec, row_spec],
        out_specs=[row_spec, row_spec],
        out_shape=[jax.ShapeDtypeStruct((rows, d), F32)] * 2,
        scratch_shapes=_conv_scratch(tm, d),
        compiler_params=_params("arbitrary"),
    )(x, mod, g_norm4, w_in, w_k, w_out, prev1, prev2)
    return x_new, u.reshape(n_seq, seq, d)[:, seq - 2:]


def _mla_proj_kernel(*refs, dims, absorb):
    q_lora, kv_lora, rope, n_heads, nope = dims
    if absorb:
        (x_ref, mod_ref, g_ref, win_ref, gq_ref, gkv_ref, wuq_ref, ck_ref, sk_ref, cq_ref, sq_ref, wukt_ref,
         ql_ref, qr_ref, ckv_ref, kr_ref) = refs
    else:
        (x_ref, mod_ref, g_ref, win_ref, gq_ref, gkv_ref, wuq_ref, ck_ref, sk_ref, wuk_ref, wuvt_ref,
         q_ref, k_ref, vt_ref, ckv_ref, kr_ref) = refs
    def rope_pair(a):
        return a * ck_ref[...] + pltpu.roll(a, LANES // 2, axis=1) * sk_ref[...]

    h = _modulated(x_ref[...], g_ref[...], mod_ref, 1).astype(BF16)
    dn = _dot(h, win_ref[...])
    cq = _rmsnorm(dn[:, :q_lora], gq_ref[...]).astype(BF16)
    ckv = _rmsnorm(dn[:, q_lora:q_lora + kv_lora], gkv_ref[...])
    ckv_ref[...] = ckv
    o = q_lora + kv_lora
    krp = rope_pair(dn[:, o:o + LANES])
    kr_ref[...] = krp[:, :rope]
    q = _dot(cq, wuq_ref[...])
    hn = n_heads * nope
    if absorb:
        hr = n_heads * rope
        qr_ref[...] = (q[:, hn:hn + hr] * cq_ref[...] + q[:, hn + hr:hn + 2 * hr] * sq_ref[...]).astype(BF16)
        for i in range(n_heads):
            qn_i = q[:, i * nope:(i + 1) * nope].astype(BF16)
            ql_ref[:, i * kv_lora:(i + 1) * kv_lora] = _dot(qn_i, wukt_ref[i]).astype(BF16)
    else:
        ckv_b = ckv.astype(BF16)
        kn = _dot(ckv_b, wuk_ref[...])
        krp_b = krp.astype(BF16)
        w = nope + LANES
        for i in range(n_heads):
            q_ref[:, i * w:i * w + nope] = q[:, i * nope:(i + 1) * nope].astype(BF16)
            q_ref[:, i * w + nope:(i + 1) * w] = rope_pair(q[:, hn + i * LANES:hn + (i + 1) * LANES]).astype(BF16)
            k_ref[:, i * w:i * w + nope] = kn[:, i * nope:(i + 1) * nope].astype(BF16)
            k_ref[:, i * w + nope:(i + 1) * w] = krp_b
        vt_ref[...] = _dot_nt(wuvt_ref[...], ckv_b).astype(BF16)


def _rope_tables(pos, rope, width, reps):
    half = rope // 2
    inv = ROPE_THETA ** (-jnp.arange(half, dtype=F32) * (2.0 / rope))
    ang = pos.astype(F32)[:, None] * inv[None, :]
    cos, sin = jnp.cos(ang), jnp.sin(ang)
    pad = jnp.zeros((pos.shape[0], width - rope), F32)
    c = jnp.concatenate([cos, cos, pad], axis=1)
    s = jnp.concatenate([-sin, sin, pad], axis=1)
    return jnp.tile(c, (1, reps)), jnp.tile(s, (1, reps))


def _swap_halves(w, rope):
    half = rope // 2
    return jnp.concatenate([w[..., half:], w[..., :half]], axis=-1)


def _mla_weights(w_in, w_uq, q_lora, kv_lora, rope, nope, absorb):
    assert 2 * rope == LANES
    n_heads = w_uq.shape[1]
    w_kr = w_in[:, q_lora + kv_lora:]
    w_in_ext = jnp.concatenate([w_in[:, :q_lora + kv_lora], w_kr, _swap_halves(w_kr, rope)], axis=1)
    w_qn = w_uq[:, :, :nope].reshape(q_lora, n_heads * nope)
    w_qr = w_uq[:, :, nope:]
    w_qr_swapped = _swap_halves(w_qr, rope)
    if absorb:
        w_rope = [w_qr.reshape(q_lora, n_heads * rope), w_qr_swapped.reshape(q_lora, n_heads * rope)]
    else:
        w_rope = [jnp.concatenate([w_qr, w_qr_swapped], axis=-1).reshape(q_lora, n_heads * LANES)]
    w_uq_ext = jnp.concatenate([w_qn] + w_rope, axis=1)
    return w_in_ext.astype(BF16), w_uq_ext.astype(BF16)


def _mla_project(x, mod, g_norm4, layer, w_in, g_q, g_kv, w_uq, w_uk, w_uv, pos_rows, rows_per_pos_table, absorb):
    rows, d = x.shape
    q_lora, kv_lora = g_q.shape[0], g_kv.shape[0]
    rope = w_in.shape[1] - q_lora - kv_lora
    n_heads, nope = w_uk.shape[1], w_uk.shape[2]
    v_dim = w_uv.shape[2]
    w_in_ext, w_uq_ext = _mla_weights(w_in, w_uq, q_lora, kv_lora, rope, nope, absorb)
    cos_k, sin_k = _rope_tables(pos_rows, rope, LANES, 1)
    period = rows_per_pos_table
    tm = _tile(period, 512)
    tiles_per_period = period // tm
    row = lambda w: pl.BlockSpec((tm, w), lambda i: (i, 0))
    tab = lambda w: pl.BlockSpec((tm, w), lambda i: (i % tiles_per_period, 0))
    full = lambda a: _resident(a.shape, lambda i: (0,) * a.ndim)
    hn = n_heads * nope
    in_specs = [
        row(d), _mod_spec(mod, layer, rows, tm),
        _resident((None, None, 1, d), lambda i: (layer, 1, 0, 0)),
        full(w_in_ext), _resident((1, q_lora), lambda i: (0, 0)), _resident((1, kv_lora), lambda i: (0, 0)),
        full(w_uq_ext), tab(LANES), tab(LANES),
    ]
    args = [x, mod, g_norm4, w_in_ext, g_q.reshape(1, q_lora), g_kv.reshape(1, kv_lora), w_uq_ext, cos_k, sin_k]
    if absorb:
        hr = n_heads * rope
        cos_q, sin_q = _rope_tables(pos_rows, rope, rope, n_heads)
        w_ukt = jnp.transpose(w_uk, (1, 2, 0)).astype(BF16)
        in_specs += [tab(hr), tab(hr), full(w_ukt)]
        args += [cos_q, sin_q, w_ukt]
        outs = [(n_heads * kv_lora, BF16), (hr, BF16)]
    else:
        qk_w = n_heads * (nope + LANES)
        hv = n_heads * v_dim
        w_uk2 = w_uk.reshape(kv_lora, hn).astype(BF16)
        w_uvt = w_uv.reshape(kv_lora, hv).T.astype(BF16)
        in_specs += [full(w_uk2), full(w_uvt)]
        args += [w_uk2, w_uvt]
        outs = [(qk_w, BF16), (qk_w, BF16)]
    out_specs = [row(w) for w, _ in outs]
    out_shape = [jax.ShapeDtypeStruct((rows, w), dt) for w, dt in outs]
    if not absorb:
        out_specs.append(pl.BlockSpec((None, None, hv, tm), lambda i: (i // tiles_per_period, i % tiles_per_period, 0, 0)))
        out_shape.append(jax.ShapeDtypeStruct((rows // period, tiles_per_period, hv, tm), BF16))
    out_specs += [row(kv_lora), row(rope)]
    out_shape += [jax.ShapeDtypeStruct((rows, kv_lora), F32), jax.ShapeDtypeStruct((rows, rope), F32)]
    return pl.pallas_call(
        functools.partial(_mla_proj_kernel, dims=(q_lora, kv_lora, rope, n_heads, nope), absorb=absorb),
        grid=(rows // tm,),
        in_specs=in_specs,
        out_specs=out_specs,
        out_shape=out_shape,
        compiler_params=_params("arbitrary"),
    )(*args)


def _attn_kernel(q_ref, k_ref, vt_ref, x_ref, mod_ref, wo_ref, o_ref, oh_ref,
                 *, n_heads, heads_per_step, v_dim, scale):
    tq = q_ref.shape[0]
    tk = vt_ref.shape[-1]
    assert tq == tk and tk % (2 * LANES) == 0
    half = tk // 2
    w = q_ref.shape[1] // n_heads
    diag = pl.program_id(1)

    log2e_scale = scale * LOG2_E

    def update(c, carry, heads, key_lo, n_keys, q_lo, masked):
        rows = pl.ds(pl.multiple_of(c * tk, tk) + key_lo, n_keys)
        scores = []
        for i in heads:
            s = _dot_nt(k_ref[rows, i * w:(i + 1) * w], q_ref[q_lo:, i * w:(i + 1) * w])
            if masked:
                key = key_lo + lax.broadcasted_iota(jnp.int32, (n_keys, 1), 0)
                query = q_lo + lax.broadcasted_iota(jnp.int32, (1, tq - q_lo), 1)
                s = jnp.where(query >= key, s, -jnp.inf)
            scores.append(s)
        out = []
        for i, s, (m, l, acc) in zip(heads, scores, carry):
            m_new = jnp.maximum(m[:, q_lo:], jnp.max(s, axis=0, keepdims=True) * log2e_scale)
            alpha = jnp.exp2(m[:, q_lo:] - m_new)
            p = jnp.exp2(s * log2e_scale - m_new)
            l_new = alpha * l[:, q_lo:] + jnp.sum(p, axis=0, keepdims=True)
            values = vt_ref[c, i * v_dim:(i + 1) * v_dim, key_lo:key_lo + n_keys]
            acc_new = alpha * acc[:, q_lo:] + _dot(values, p.astype(BF16))
            if q_lo:
                m_new, l_new, acc_new = (jnp.concatenate([old[:, :q_lo], new], axis=1)
                                         for old, new in ((m, m_new), (l, l_new), (acc, acc_new)))
            out.append((m_new, l_new, acc_new))
        return tuple(out)

    init = (jnp.full((1, tq), NEG_BIG, F32), jnp.zeros((1, tq), F32), jnp.zeros((v_dim, tq), F32))
    for g in range(0, n_heads, heads_per_step):
        heads = tuple(range(g, g + heads_per_step))
        full_block = functools.partial(update, heads=heads, key_lo=0, n_keys=tk, q_lo=0, masked=False)
        carry = lax.fori_loop(0, diag, full_block, (init,) * heads_per_step)
        carry = update(diag, carry, heads, 0, half, 0, True)
        carry = update(diag, carry, heads, half, half, half, True)
        for i, (_, l, acc) in zip(heads, carry):
            oh_ref[:, i * v_dim:(i + 1) * v_dim] = (acc / l).T.astype(BF16)
    o_ref[...] = x_ref[...] + _mod(mod_ref, MIXER_GATE) * _dot(oh_ref[...], wo_ref[...])


def _attn_prompt(q, k, vt, x, mod, w_o, layer, j, n_seq, n_heads, scale):
    rows, d = x.shape
    seq = rows // n_seq
    _, n_kv, hv, tk = vt.shape
    assert n_kv * tk == seq
    tq = _tile(seq, 512)
    r3 = lambda a: a.reshape(n_seq, seq, a.shape[-1])
    q_spec = lambda w: pl.BlockSpec((None, tq, w), lambda b, i: (b, i, 0))
    out = pl.pallas_call(
        functools.partial(_attn_kernel, n_heads=n_heads, heads_per_step=_tile(n_heads, 8),
                          v_dim=hv // n_heads, scale=scale),
        grid=(n_seq, seq // tq),
        in_specs=[
            q_spec(q.shape[-1]),
            pl.BlockSpec((None, seq, k.shape[-1]), lambda b, i: (b, 0, 0)),
            pl.BlockSpec((None, n_kv, hv, tk), lambda b, i: (b, 0, 0, 0)),
            q_spec(d),
            pl.BlockSpec((None, None, 1, mod.shape[-1]), lambda b, i: (layer, b, 0, 0)),
            _resident((None, hv, d), lambda b, i: (j, 0, 0)),
        ],
        out_specs=q_spec(d),
        out_shape=jax.ShapeDtypeStruct((n_seq, seq, d), F32),
        scratch_shapes=[pltpu.VMEM((tq, hv), BF16)],
        compiler_params=_params("arbitrary", "arbitrary"),
    )(r3(q), r3(k), vt, r3(x), mod, w_o)
    return out.reshape(rows, d)


def _page_copies(pt_ref, lat_hbm, krt_hbm, lat_buf, krt_buf, sem, layer, seq, chunk, slot):
    pages_per_chunk, _, page = krt_buf.shape[1:]
    out = []
    for k in range(pages_per_chunk):
        pg = pt_ref[seq, chunk * pages_per_chunk + k]
        out.append(pltpu.make_async_copy(
            lat_hbm.at[layer, pg], lat_buf.at[slot, pl.ds(k * page, page)], sem.at[0, slot]))
        out.append(pltpu.make_async_copy(krt_hbm.at[layer, pg], krt_buf.at[slot, k], sem.at[1, slot]))
    return out


def _start_pages(copies):
    for n, cp in enumerate(copies):
        cp.start(priority=(n // 2) % 2)


def _wait_pages(copies):
    for cp in copies:
        cp.wait()


def _softmax_update(state, s, values):
    m, l, acc = state
    m_new = jnp.maximum(m, jnp.max(s, axis=-1, keepdims=True))
    alpha = jnp.exp(m - m_new)
    p = jnp.exp(s - m_new)
    l = alpha * l + jnp.sum(p, axis=-1, keepdims=True)
    acc = alpha * acc + _dot(p.astype(BF16), values)
    return m_new, l, acc


def _decode_init(n_rows, kv_lora, n_split):
    init = (jnp.full((n_rows, 1), NEG_BIG, F32), jnp.zeros((n_rows, 1), F32), jnp.zeros((n_rows, kv_lora), F32))
    return (init,) * n_split


def _decode_part(k, n_split, ql, qr, lat_buf, krt_buf, slot, scale):
    page = krt_buf.shape[-1]
    sub = lat_buf.shape[1] // n_split
    lat = lat_buf[slot, k * sub:(k + 1) * sub, :].astype(BF16)
    pages = range(k * sub // page, (k + 1) * sub // page)
    krt = jnp.concatenate([krt_buf[slot, pg].astype(BF16) for pg in pages], axis=1)
    return lat, (_dot_nt(ql, lat) + _dot(qr, krt)) * scale


def _decode_chunk(states, ql, qr, lat_buf, krt_buf, slot, scale):
    parts = [_decode_part(k, len(states), ql, qr, lat_buf, krt_buf, slot, scale) for k in range(len(states))]
    return tuple(_softmax_update(state, s, lat) for state, (lat, s) in zip(states, parts))


def _decode_finish(states, ql, qr, cn, kn, n_heads, n_new, scale):
    m = functools.reduce(jnp.maximum, [st[0] for st in states])
    weights = [jnp.exp(st[0] - m) for st in states]
    merged = (m, sum(w * st[1] for w, st in zip(weights, states)), sum(w * st[2] for w, st in zip(weights, states)))
    s_new = (_dot_nt(ql, cn) + _dot_nt(qr, kn)) * scale
    t = lax.broadcasted_iota(jnp.int32, (ql.shape[0], 1), 0) // n_heads
    u = lax.broadcasted_iota(jnp.int32, (1, cn.shape[0]), 1)
    s_new = jnp.where((u <= t) & (u < n_new), s_new, -jnp.inf)
    _, l, acc = _softmax_update(merged, s_new, cn)
    return acc / l


def _decode_kernel(pt_ref, ql_ref, qr_ref, cn_ref, kn_ref, lat_hbm, krt_hbm, o_ref, lat_buf, krt_buf, sem,
                   *, layer, n_chunks, n_heads, n_new, n_split, scale):
    b = pl.program_id(0)
    n_b = pl.num_programs(0)
    copies = functools.partial(_page_copies, pt_ref, lat_hbm, krt_hbm, lat_buf, krt_buf, sem, layer)

    @pl.when(b == 0)
    def _():
        _start_pages(copies(0, 0, 0))

    ql = ql_ref[...]
    qr = qr_ref[...]

    def step(c, states):
        slot = (b * n_chunks + c) % 2
        last = c + 1 == n_chunks
        nb = jnp.where(last, b + 1, b)
        nc = jnp.where(last, 0, c + 1)

        @pl.when(nb < n_b)
        def _():
            _start_pages(copies(nb, nc, 1 - slot))

        _wait_pages(copies(b, c, slot))
        return _decode_chunk(states, ql, qr, lat_buf, krt_buf, slot, scale)

    states = lax.fori_loop(0, n_chunks, step, _decode_init(ql.shape[0], lat_buf.shape[-1], n_split))
    o_ref[...] = _decode_finish(states, ql, qr, cn_ref[...], kn_ref[...], n_heads, n_new, scale)


def _decode_setup(dec):
    page_table, q_lat, q_rope, ckv_new, kr_new, cache_lat, cache_krt, cache_layer, n_heads, scale = dec
    _, n_rows, kv_lora = q_lat.shape
    rope = q_rope.shape[-1]
    n_new = n_rows // n_heads
    page = cache_lat.shape[2]
    n_pages = page_table.shape[1]
    pages_per_chunk = _tile(n_pages, 32)
    assert n_new <= LANES
    pad_keys = lambda a: jnp.pad(a.astype(BF16), ((0, 0), (0, LANES - n_new), (0, 0)))
    operands = [q_lat, q_rope, pad_keys(ckv_new), pad_keys(kr_new), cache_lat, cache_krt]
    block_dims = [(n_rows, kv_lora), (n_rows, rope), (LANES, kv_lora), (LANES, rope)]
    scratch = [
        pltpu.VMEM((2, pages_per_chunk * page, kv_lora), F32),
        pltpu.VMEM((2, pages_per_chunk, rope, page), F32),
        pltpu.SemaphoreType.DMA((2, 2)),
    ]
    static = dict(layer=cache_layer, n_chunks=n_pages // pages_per_chunk, n_heads=n_heads, n_new=n_new,
                  n_split=_tile(pages_per_chunk, 4), scale=scale)
    return page_table, operands, block_dims, scratch, static


def _decode_attn(dec):
    page_table, operands, block_dims, scratch, static = _decode_setup(dec)
    n_b, n_rows, kv_lora = operands[0].shape
    per_b = lambda dims: pl.BlockSpec((None,) + dims, lambda b, pt: (b, 0, 0))
    any_space = pl.BlockSpec(memory_space=pl.ANY)
    return pl.pallas_call(
        functools.partial(_decode_kernel, **static),
        grid_spec=pltpu.PrefetchScalarGridSpec(
            num_scalar_prefetch=1,
            grid=(n_b,),
            in_specs=[per_b(dims) for dims in block_dims] + [any_space, any_space],
            out_specs=per_b((n_rows, kv_lora)),
            scratch_shapes=scratch,
        ),
        out_shape=jax.ShapeDtypeStruct((n_b, n_rows, kv_lora), F32),
        compiler_params=_params("arbitrary"),
    )(page_table, *operands)


def _ffn_decode_kernel(pt_ref, *refs, s, tf, final_norm, seq0, layer, n_chunks, n_heads, n_new, n_split, scale):
    if final_norm:
        x_ref, mod_ref, g_ref, wgu_ref, wd_ref, gf_ref, *refs = refs
    else:
        x_ref, mod_ref, g_ref, wgu_ref, wd_ref, *refs = refs
        gf_ref = None
    ql_ref, qr_ref, cn_ref, kn_ref, lat_hbm, krt_hbm, o_ref, ol_ref, h_ref, a_ref, lat_buf, krt_buf, sem = refs
    i = pl.program_id(0)
    n = pl.num_programs(0)
    seqs = ql_ref.shape[0]
    first = seq0 + i * seqs
    copies = functools.partial(_page_copies, pt_ref, lat_hbm, krt_hbm, lat_buf, krt_buf, sem, layer)
    items = [(q, c) for q in range(seqs) for c in range(n_chunks)]
    assert len(items) % 2 == 0

    @pl.when(i == 0)
    def _():
        _start_pages(copies(first, 0, 0))

    states = {}

    def chunk_step(w):
        q, c = items[w]
        if w + 1 < len(items):
            _start_pages(copies(first + items[w + 1][0], items[w + 1][1], (w + 1) % 2))
        else:
            _start_pages(copies(jnp.where(i + 1 < n, first + seqs, seq0), 0, 0))
        _wait_pages(copies(first + q, c, w % 2))
        ql, qr = ql_ref[q], qr_ref[q]
        prev = _decode_init(ql.shape[0], lat_buf.shape[-1], n_split) if c == 0 else states[q]
        states[q] = _decode_chunk(prev, ql, qr, lat_buf, krt_buf, w % 2, scale)
        if c == n_chunks - 1:
            ol_ref[q] = _decode_finish(states[q], ql, qr, cn_ref[q], kn_ref[q], n_heads, n_new, scale)

    n_f = wd_ref.shape[0] // tf
    after = {}
    for w in range(len(items)):
        after.setdefault(2 * min(n_f - 1, (2 * w + 1) * n_f // (2 * len(items))) + 1, []).append(w)

    def after_dot(gap):
        for w in after.get(gap, ()):
            chunk_step(w)

    _ffn_tile(x_ref, mod_ref, g_ref, wgu_ref, wd_ref, gf_ref, o_ref, h_ref, a_ref, s, tf, after_dot)

    @pl.when(i == n - 1)
    def _():
        _wait_pages(copies(seq0, 0, 0))


def _decode_fits_ffn_calls(dec, n_calls, x_rows, mod):
    page_table, q_lat = dec[0], dec[1]
    n_steps = x_rows // _tile(_rows_per_mod(mod, x_rows), 512)
    n_chunks = page_table.shape[1] // _tile(page_table.shape[1], 32)
    n_seq = q_lat.shape[0]
    return n_seq % (n_calls * n_steps) == 0 and (n_seq // (n_calls * n_steps) * n_chunks) % 2 == 0


def _ffn_with_decode(x, mod, g_norm4, w_gu, w_down, layer, k, s, g_final, dec, seq0, n_seq):
    rows, d = x.shape
    f = w_down.shape[2]
    tm, tf, in_specs, args = _ffn_operands(x, mod, g_norm4, w_gu, w_down, layer, k, s, g_final)
    page_table, operands, block_dims, scratch, static = _decode_setup(dec)
    n_steps = rows // tm
    seqs = n_seq // n_steps
    assert seqs * n_steps == n_seq and seq0 % seqs == 0
    per_step = lambda dims, blk0: pl.BlockSpec((seqs,) + dims, lambda i, pt: (blk0 + i, 0, 0))
    any_space = pl.BlockSpec(memory_space=pl.ANY)
    n_rows, kv_lora = block_dims[0]
    return pl.pallas_call(
        functools.partial(_ffn_decode_kernel, s=s, tf=tf, final_norm=g_final is not None, seq0=seq0, **static),
        grid_spec=pltpu.PrefetchScalarGridSpec(
            num_scalar_prefetch=1,
            grid=(n_steps,),
            in_specs=in_specs + [per_step(dims, seq0 // seqs) for dims in block_dims] + [any_space, any_space],
            out_specs=[pl.BlockSpec((tm, d), lambda i, pt: (i, 0)), per_step((n_rows, kv_lora), 0)],
            scratch_shapes=[pltpu.VMEM((tm, d), BF16), pltpu.VMEM((tm, f), BF16)] + scratch,
        ),
        out_shape=[jax.ShapeDtypeStruct((rows, d), F32), jax.ShapeDtypeStruct((n_seq, n_rows, kv_lora), F32)],
        compiler_params=_params("arbitrary"),
    )(page_table, *args, *operands)


def _decode_out_kernel(ol_ref, x_ref, mod_ref, wuv_ref, wo_ref, o_ref, oh_ref, *, n_heads, kv_lora, v_dim):
    for i in range(n_heads):
        o_i = ol_ref[:, i * kv_lora:(i + 1) * kv_lora].astype(BF16)
        oh_ref[:, i * v_dim:(i + 1) * v_dim] = _dot(o_i, wuv_ref[i]).astype(BF16)
    o_ref[...] = x_ref[...] + _mod(mod_ref, MIXER_GATE) * _dot(oh_ref[...], wo_ref[...])


def _decode_out(o_lat, x, mod, w_uv, w_o, layer, j):
    rows, d = x.shape
    kv_lora, n_heads, v_dim = w_uv.shape
    w_uv_h = jnp.transpose(w_uv, (1, 0, 2)).astype(BF16)
    tm = _tile(rows, 512)
    row = lambda w: pl.BlockSpec((tm, w), lambda i: (i, 0))
    return pl.pallas_call(
        functools.partial(_decode_out_kernel, n_heads=n_heads, kv_lora=kv_lora, v_dim=v_dim),
        grid=(rows // tm,),
        in_specs=[
            row(n_heads * kv_lora), row(d), _mod_spec(mod, layer, rows, tm),
            _resident(w_uv_h.shape, lambda i: (0, 0, 0)),
            _resident((None, n_heads * v_dim, d), lambda i: (j, 0, 0)),
        ],
        out_specs=row(d),
        out_shape=jax.ShapeDtypeStruct((rows, d), F32),
        scratch_shapes=[pltpu.VMEM((tm, n_heads * v_dim), BF16)],
        compiler_params=_params("arbitrary"),
    )(o_lat, x, mod, w_uv_h, w_o)


def kernel(x_prompt, x_sample, c_prompt, c_sample, state_conv, cache_kv_latent, cache_k_rope, page_table, w_ada, b_ada, g_norm, w_ffn_gu, w_ffn_down, w_conv_in, w_conv_k, w_conv_out, w_mla_in, g_q_norm, g_kv_norm, w_uq, w_uk, w_uv, w_mla_out, g_final):
    n_p, seq_p, d = x_prompt.shape
    n_s, seq_s, _ = x_sample.shape
    depth = w_ada.shape[0]
    past = page_table.shape[1] * cache_kv_latent.shape[2]
    n_heads, qk_dim = w_uq.shape[2], w_uq.shape[3]
    v_dim = w_uv.shape[3]
    kv_lora = g_kv_norm.shape[1]
    scale = qk_dim ** -0.5

    xp = x_prompt.reshape(n_p * seq_p, d)
    xs = x_sample.reshape(n_s * seq_s, d)
    mp, ms = _adaln(c_prompt, jnp.repeat(c_sample, seq_s, axis=0), w_ada, b_ada)
    mp = mp.reshape(depth, n_p, 1, N_MOD * d)
    g_norm4 = g_norm.reshape(depth, N_SUB, 1, d)
    w_gu = w_ffn_gu.astype(BF16)
    w_down = w_ffn_down.astype(BF16)
    w_cin = w_conv_in.astype(BF16)
    w_cout = w_conv_out.astype(BF16)
    w_mo = w_mla_out.astype(BF16)
    cache_krt = jnp.swapaxes(cache_k_rope, 2, 3)

    conv_p, conv_s, lat_p, kr_p, lat_s, kr_s = [], [], [], [], [], []
    for i in range(depth):
        gf = g_final if i == depth - 1 else None
        ffn_a = lambda x, m: _ffn(x, m, g_norm4, w_gu, w_down, i, 0, 0)
        ffn_b = lambda x, m: _ffn(x, m, g_norm4, w_gu, w_down, i, 1, 2, gf)
        j = i // N_MIXERS
        if i % N_MIXERS == 0:
            xp, xs = ffn_a(xp, mp), ffn_a(xs, ms)
            xp, st_p = _conv_prompt(xp, mp, g_norm4, w_cin, w_conv_k, w_cout, i, j, n_p)
            xs, st_s = _conv_sample(xs, ms, g_norm4, w_cin, w_conv_k, w_cout, state_conv[j], i, j)
            conv_p.append(st_p)
            conv_s.append(st_s)
            xp, xs = ffn_b(xp, mp), ffn_b(xs, ms)
            continue
        proj = (w_mla_in[j], g_q_norm[j], g_kv_norm[j], w_uq[j], w_uk[j], w_uv[j])
        xs = ffn_a(xs, ms)
        pos_s = jnp.tile(past + jnp.arange(seq_s, dtype=jnp.int32), n_s)
        ql, qrs, ckv_s, k_s = _mla_project(xs, ms, g_norm4, i, *proj, pos_s, n_s * seq_s, absorb=True)
        dec = (page_table, ql.reshape(n_s, seq_s * n_heads, kv_lora), qrs.reshape(n_s, seq_s * n_heads, -1),
               ckv_s.reshape(n_s, seq_s, -1), k_s.reshape(n_s, seq_s, -1), cache_kv_latent, cache_krt, j, n_heads, scale)
        fused = _decode_fits_ffn_calls(dec, 2, xp.shape[0], mp)
        half = n_s // 2
        if fused:
            xp, o_lat_a = _ffn_with_decode(xp, mp, g_norm4, w_gu, w_down, i, 0, 0, None, dec, 0, half)
        else:
            xp = ffn_a(xp, mp)
        q, k, vt, ckv_p, k_p = _mla_project(
            xp, mp, g_norm4, i, *proj, jnp.arange(seq_p, dtype=jnp.int32), seq_p, absorb=False)
        xp = _attn_prompt(q, k, vt, xp, mp, w_mo, i, j, n_p, n_heads, scale)
        if fused:
            xp, o_lat_b = _ffn_with_decode(xp, mp, g_norm4, w_gu, w_down, i, 1, 2, gf, dec, half, half)
            o_lat = jnp.concatenate([o_lat_a, o_lat_b], axis=0)
        else:
            xp = ffn_b(xp, mp)
            o_lat = _decode_attn(dec)
        xs = _decode_out(o_lat.reshape(n_s * seq_s, n_heads * kv_lora), xs, ms, w_uv[j], w_mo, i, j)
        xs = ffn_b(xs, ms)
        lat_p.append(ckv_p.reshape(n_p, seq_p, -1))
        kr_p.append(k_p.reshape(n_p, seq_p, -1))
        lat_s.append(ckv_s.reshape(n_s, seq_s, -1))
        kr_s.append(k_s.reshape(n_s, seq_s, -1))
    return (xp.reshape(n_p, seq_p, d), xs.reshape(n_s, seq_s, d), jnp.stack(conv_p), jnp.stack(conv_s),
            jnp.stack(lat_p), jnp.stack(kr_p), jnp.stack(lat_s), jnp.stack(kr_s))
```

```python
import functools

import jax
import jax.numpy as jnp
from jax import lax
from jax.experimental import pallas as pl
from jax.experimental.pallas import tpu as pltpu

EPS = 1e-6
ROPE_THETA = 10000.0
N_SUB = 3
N_MOD = 3 * N_SUB
N_MIXERS = 2
MIXER_GATE = 3 * 1 + 2

BF16 = jnp.bfloat16
F32 = jnp.float32

LANES = 128
VMEM_LIMIT_BYTES = 56 * 1024 * 1024
NEG_BIG = -1e30
LOG2_E = 1.4426950408889634

ROW_TILE = 512
HIDDEN_CHUNK = 256
ADALN_COLS = 1024
ATTN_HEADS_PER_STEP = 8
PAGES_PER_CHUNK = 32
KEY_PARTS = 4


def _params(*semantics):
    return pltpu.CompilerParams(dimension_semantics=semantics, vmem_limit_bytes=VMEM_LIMIT_BYTES)


def _dot(a, b):
    return jnp.dot(a, b, preferred_element_type=F32)


def _dot_nt(a, b):
    return lax.dot_general(a, b, (((1,), (1,)), ((), ())), preferred_element_type=F32)


def _rmsnorm(x, g):
    return x * lax.rsqrt(jnp.mean(x * x, axis=-1, keepdims=True) + EPS) * g


def _mod(mod_ref, k):
    d = mod_ref.shape[-1] // N_MOD
    return mod_ref[:, k * d:(k + 1) * d]


def _modulated(x, g, mod_ref, s):
    return _rmsnorm(x, g) * (1.0 + _mod(mod_ref, 3 * s + 1)) + _mod(mod_ref, 3 * s)


def _tile(n, want):
    t = min(n, want)
    while n % t:
        t -= 1
    return t


def _resident(block_shape, index_map):
    return pl.BlockSpec(block_shape, index_map, pipeline_mode=pl.Buffered(1))


def _rows_per_mod(mod, rows):
    return rows // mod.shape[1] if mod.ndim == 4 else 1


def _mod_spec(mod, layer, rows, tm):
    width = mod.shape[-1]
    if mod.ndim == 4:
        rows_per_seq = _rows_per_mod(mod, rows)
        assert rows_per_seq % tm == 0
        tiles_per_seq = rows_per_seq // tm
        return pl.BlockSpec((None, None, 1, width), lambda i, *_: (layer, i // tiles_per_seq, 0, 0))
    assert mod.shape[1] == rows
    return pl.BlockSpec((None, tm, width), lambda i, *_: (layer, i, 0))


def _adaln_kernel(cp_ref, cs_ref, w_ref, b_ref, op_ref, os_ref):
    w = w_ref[...].astype(BF16)
    for c_ref, o_ref in ((cp_ref, op_ref), (cs_ref, os_ref)):
        c = c_ref[...]
        o_ref[...] = _dot((c * jax.nn.sigmoid(c)).astype(BF16), w) + b_ref[...]


def _adaln(c_p, c_s, w_ada, b_ada):
    depth, d, n_out = w_ada.shape
    tn = _tile(n_out, ADALN_COLS)
    c_spec = lambda c: _resident(c.shape, lambda l, j: (0, 0))
    o_spec = lambda c: pl.BlockSpec((None, c.shape[0], tn), lambda l, j: (l, 0, j))
    return pl.pallas_call(
        _adaln_kernel,
        grid=(depth, n_out // tn),
        in_specs=[
            c_spec(c_p), c_spec(c_s),
            pl.BlockSpec((None, d, tn), lambda l, j: (l, 0, j)),
            pl.BlockSpec((None, 1, tn), lambda l, j: (l, 0, j)),
        ],
        out_specs=[o_spec(c_p), o_spec(c_s)],
        out_shape=[jax.ShapeDtypeStruct((depth, c.shape[0], n_out), F32) for c in (c_p, c_s)],
        compiler_params=_params("arbitrary", "arbitrary"),
    )(c_p, c_s, w_ada, b_ada.reshape(depth, 1, n_out))


def _ffn_tile(x_ref, mod_ref, g_ref, wgu_ref, wd_ref, gf_ref, o_ref, h_ref, a_ref, s, tf, after_dot=None):
    f = wd_ref.shape[0]
    after_dot = after_dot or (lambda n: None)
    h_ref[...] = _modulated(x_ref[...], g_ref[...], mod_ref, s).astype(BF16)
    for c in range(f // tf):
        gate = _dot(h_ref[...], wgu_ref[:, c * tf:(c + 1) * tf])
        after_dot(2 * c)
        up = _dot(h_ref[...], wgu_ref[:, f + c * tf:f + (c + 1) * tf])
        a_ref[:, c * tf:(c + 1) * tf] = (gate * jax.nn.sigmoid(gate) * up).astype(BF16)
        after_dot(2 * c + 1)
    xn = x_ref[...] + 0.5 * _mod(mod_ref, 3 * s + 2) * _dot(a_ref[...], wd_ref[...])
    if gf_ref is not None:
        xn = _rmsnorm(xn, gf_ref[...])
    o_ref[...] = xn


def _ffn_kernel(*refs, s, tf, final_norm):
    if final_norm:
        x_ref, mod_ref, g_ref, wgu_ref, wd_ref, gf_ref, o_ref, h_ref, a_ref = refs
    else:
        x_ref, mod_ref, g_ref, wgu_ref, wd_ref, o_ref, h_ref, a_ref = refs
        gf_ref = None
    _ffn_tile(x_ref, mod_ref, g_ref, wgu_ref, wd_ref, gf_ref, o_ref, h_ref, a_ref, s, tf)


def _ffn_operands(x, mod, g_norm4, w_gu, w_down, layer, k, s, g_final):
    rows, d = x.shape
    f = w_down.shape[2]
    tm = _tile(rows if mod.ndim == 3 else _rows_per_mod(mod, rows), ROW_TILE)
    in_specs = [
        pl.BlockSpec((tm, d), lambda i, *_: (i, 0)),
        _mod_spec(mod, layer, rows, tm),
        _resident((None, None, 1, d), lambda i, *_: (layer, s, 0, 0)),
        _resident((None, None, d, 2 * f), lambda i, *_: (layer, k, 0, 0)),
        _resident((None, None, f, d), lambda i, *_: (layer, k, 0, 0)),
    ]
    args = [x, mod, g_norm4, w_gu, w_down]
    if g_final is not None:
        in_specs.append(_resident((1, d), lambda i, *_: (0, 0)))
        args.append(g_final.reshape(1, d))
    return tm, _tile(f, HIDDEN_CHUNK), in_specs, args


def _ffn(x, mod, g_norm4, w_gu, w_down, layer, k, s, g_final=None):
    rows, d = x.shape
    f = w_down.shape[2]
    tm, tf, in_specs, args = _ffn_operands(x, mod, g_norm4, w_gu, w_down, layer, k, s, g_final)
    return pl.pallas_call(
        functools.partial(_ffn_kernel, s=s, tf=tf, final_norm=g_final is not None),
        grid=(rows // tm,),
        in_specs=in_specs,
        out_specs=pl.BlockSpec((tm, d), lambda i: (i, 0)),
        out_shape=jax.ShapeDtypeStruct((rows, d), F32),
        scratch_shapes=[pltpu.VMEM((tm, d), BF16), pltpu.VMEM((tm, f), BF16)],
        compiler_params=_params("arbitrary"),
    )(*args)


CONV_COLS = 256


def _conv_mixer(x_ref, mod_ref, g_ref, win_ref, wk_ref, wout_ref, o_ref, h_ref, bc_ref, shifted, keep):
    d = x_ref.shape[-1]
    h_ref[...] = _modulated(x_ref[...], g_ref[...], mod_ref, 1).astype(BF16)
    for c in range(0, d, CONV_COLS):
        cols = slice(c, c + CONV_COLS)
        proj = lambda part: _dot(h_ref[...], win_ref[:, part * d + c:part * d + c + CONV_COLS])
        b_gate = proj(0)
        u = proj(1) * proj(2)
        u1, u2 = shifted(u, cols)
        conv = wk_ref[0:1, cols] * u2 + wk_ref[1:2, cols] * u1 + wk_ref[2:3, cols] * u
        bc_ref[:, cols] = (b_gate * conv).astype(BF16)
        keep(u, cols)
    o_ref[...] = x_ref[...] + _mod(mod_ref, MIXER_GATE) * _dot(bc_ref[...], wout_ref[...])


def _conv_seq_kernel(x_ref, mod_ref, g_ref, win_ref, wk_ref, wout_ref, o_ref, st_ref, h_ref, bc_ref, carry_ref,
                     *, tiles_per_seq):
    tm = x_ref.shape[0]

    @pl.when(pl.program_id(0) % tiles_per_seq == 0)
    def _():
        carry_ref[...] = jnp.zeros_like(carry_ref)

    row = lax.broadcasted_iota(jnp.int32, (tm, 1), 0)

    def shifted(u, cols):
        prev0 = carry_ref[0:1, cols]
        prev1 = carry_ref[1:2, cols]
        u1 = jnp.where(row == 0, prev1, pltpu.roll(u, 1, axis=0))
        u2 = jnp.where(row == 0, prev0, jnp.where(row == 1, prev1, pltpu.roll(u, 2, axis=0)))
        return u1, u2

    def keep(u, cols):
        carry_ref[:, cols] = u[tm - 2:, :]
        st_ref[:, cols] = u[tm - 2:, :]

    _conv_mixer(x_ref, mod_ref, g_ref, win_ref, wk_ref, wout_ref, o_ref, h_ref, bc_ref, shifted, keep)


def _conv_step_kernel(x_ref, mod_ref, g_ref, win_ref, wk_ref, wout_ref, p1_ref, p2_ref, o_ref, u_ref, h_ref, bc_ref,
                      *, seq):
    tm = x_ref.shape[0]
    t = lax.broadcasted_iota(jnp.int32, (tm, 1), 0) % seq

    def shifted(u, cols):
        u1 = jnp.where(t >= 1, pltpu.roll(u, 1, axis=0), p1_ref[:, cols])
        u2 = jnp.where(t >= 2, pltpu.roll(u, 2, axis=0), p2_ref[:, cols])
        return u1, u2

    def keep(u, cols):
        u_ref[:, cols] = u

    _conv_mixer(x_ref, mod_ref, g_ref, win_ref, wk_ref, wout_ref, o_ref, h_ref, bc_ref, shifted, keep)


def _conv_scratch(tm, d):
    assert d % CONV_COLS == 0
    return [pltpu.VMEM((tm, d), BF16), pltpu.VMEM((tm, d), BF16)]


def _conv_weight_specs(d, n_taps, layer, j):
    return [
        _resident((None, None, 1, d), lambda i: (layer, 1, 0, 0)),
        _resident((None, d, 3 * d), lambda i: (j, 0, 0)),
        _resident((None, n_taps, d), lambda i: (j, 0, 0)),
        _resident((None, d, d), lambda i: (j, 0, 0)),
    ]


def _conv_prompt(x, mod, g_norm4, w_in, w_k, w_out, layer, j, n_seq):
    rows, d = x.shape
    seq = rows // n_seq
    assert w_k.shape[1] == 3 and seq >= 2
    tm = _tile(seq, ROW_TILE)
    assert tm >= 2
    tiles_per_seq = seq // tm
    return pl.pallas_call(
        functools.partial(_conv_seq_kernel, tiles_per_seq=tiles_per_seq),
        grid=(rows // tm,),
        in_specs=[pl.BlockSpec((tm, d), lambda i: (i, 0)), _mod_spec(mod, layer, rows, tm)]
        + _conv_weight_specs(d, 3, layer, j),
        out_specs=[
            pl.BlockSpec((tm, d), lambda i: (i, 0)),
            pl.BlockSpec((None, 2, d), lambda i: (i // tiles_per_seq, 0, 0)),
        ],
        out_shape=[jax.ShapeDtypeStruct((rows, d), F32), jax.ShapeDtypeStruct((n_seq, 2, d), F32)],
        scratch_shapes=_conv_scratch(tm, d) + [pltpu.VMEM((2, d), F32)],
        compiler_params=_params("arbitrary"),
    )(x, mod, g_norm4, w_in, w_k, w_out)


def _conv_sample(x, mod, g_norm4, w_in, w_k, w_out, state, layer, j):
    rows, d = x.shape
    n_seq = state.shape[0]
    seq = rows // n_seq
    assert w_k.shape[1] == 3 and seq >= 2
    zeros = lambda n: jnp.zeros((n_seq, n, d), F32)
    prev1 = jnp.concatenate([state[:, 1:2], zeros(seq - 1)], axis=1).reshape(rows, d)
    prev2 = jnp.concatenate([state, zeros(seq - 2)], axis=1).reshape(rows, d)
    seqs_per_tile = _tile(n_seq, max(1, ROW_TILE // seq))
    tm = seqs_per_tile * seq
    row_spec = pl.BlockSpec((tm, d), lambda i: (i, 0))
    x_new, u = pl.pallas_call(
        functools.partial(_conv_step_kernel, seq=seq),
        grid=(rows // tm,),
        in_specs=[row_spec, _mod_spec(mod, layer, rows, tm)] + _conv_weight_specs(d, 3, layer, j) + [row_spec, row_spec],
        out_specs=[row_spec, row_spec],
        out_shape=[jax.ShapeDtypeStruct((rows, d), F32)] * 2,
        scratch_shapes=_conv_scratch(tm, d),
        compiler_params=_params("arbitrary"),
    )(x, mod, g_norm4, w_in, w_k, w_out, prev1, prev2)
    return x_new, u.reshape(n_seq, seq, d)[:, seq - 2:]


def _mla_proj_kernel(*refs, dims, absorb):
    q_lora, kv_lora, rope, n_heads, nope = dims
    if absorb:
        (x_ref, mod_ref, g_ref, win_ref, gq_ref, gkv_ref, wuq_ref, ck_ref, sk_ref, cq_ref, sq_ref, wukt_ref,
         ql_ref, qr_ref, ckv_ref, kr_ref) = refs
    else:
        (x_ref, mod_ref, g_ref, win_ref, gq_ref, gkv_ref, wuq_ref, ck_ref, sk_ref, wuk_ref, wuvt_ref,
         q_ref, k_ref, vt_ref, ckv_ref, kr_ref) = refs
    def rope_pair(a):
        return a * ck_ref[...] + pltpu.roll(a, LANES // 2, axis=1) * sk_ref[...]

    h = _modulated(x_ref[...], g_ref[...], mod_ref, 1).astype(BF16)
    dn = _dot(h, win_ref[...])
    cq = _rmsnorm(dn[:, :q_lora], gq_ref[...]).astype(BF16)
    ckv = _rmsnorm(dn[:, q_lora:q_lora + kv_lora], gkv_ref[...])
    ckv_ref[...] = ckv
    o = q_lora + kv_lora
    krp = rope_pair(dn[:, o:o + LANES])
    kr_ref[...] = krp[:, :rope]
    q = _dot(cq, wuq_ref[...])
    hn = n_heads * nope
    if absorb:
        hr = n_heads * rope
        qr_ref[...] = (q[:, hn:hn + hr] * cq_ref[...] + q[:, hn + hr:hn + 2 * hr] * sq_ref[...]).astype(BF16)
        for i in range(n_heads):
            qn_i = q[:, i * nope:(i + 1) * nope].astype(BF16)
            ql_ref[:, i * kv_lora:(i + 1) * kv_lora] = _dot(qn_i, wukt_ref[i]).astype(BF16)
    else:
        ckv_b = ckv.astype(BF16)
        kn = _dot(ckv_b, wuk_ref[...])
        krp_b = krp.astype(BF16)
        w = nope + LANES
        for i in range(n_heads):
            q_ref[:, i * w:i * w + nope] = q[:, i * nope:(i + 1) * nope].astype(BF16)
            q_ref[:, i * w + nope:(i + 1) * w] = rope_pair(q[:, hn + i * LANES:hn + (i + 1) * LANES]).astype(BF16)
            k_ref[:, i * w:i * w + nope] = kn[:, i * nope:(i + 1) * nope].astype(BF16)
            k_ref[:, i * w + nope:(i + 1) * w] = krp_b
        vt_ref[...] = _dot_nt(wuvt_ref[...], ckv_b).astype(BF16)


def _rope_tables(pos, rope, width, reps):
    half = rope // 2
    inv = ROPE_THETA ** (-jnp.arange(half, dtype=F32) * (2.0 / rope))
    ang = pos.astype(F32)[:, None] * inv[None, :]
    cos, sin = jnp.cos(ang), jnp.sin(ang)
    pad = jnp.zeros((pos.shape[0], width - rope), F32)
    c = jnp.concatenate([cos, cos, pad], axis=1)
    s = jnp.concatenate([-sin, sin, pad], axis=1)
    return jnp.tile(c, (1, reps)), jnp.tile(s, (1, reps))


def _swap_halves(w, rope):
    half = rope // 2
    return jnp.concatenate([w[..., half:], w[..., :half]], axis=-1)


def _mla_weights(w_in, w_uq, q_lora, kv_lora, rope, nope, absorb):
    assert 2 * rope == LANES
    n_heads = w_uq.shape[1]
    w_kr = w_in[:, q_lora + kv_lora:]
    w_in_ext = jnp.concatenate([w_in[:, :q_lora + kv_lora], w_kr, _swap_halves(w_kr, rope)], axis=1)
    w_qn = w_uq[:, :, :nope].reshape(q_lora, n_heads * nope)
    w_qr = w_uq[:, :, nope:]
    w_qr_swapped = _swap_halves(w_qr, rope)
    if absorb:
        w_rope = [w_qr.reshape(q_lora, n_heads * rope), w_qr_swapped.reshape(q_lora, n_heads * rope)]
    else:
        w_rope = [jnp.concatenate([w_qr, w_qr_swapped], axis=-1).reshape(q_lora, n_heads * LANES)]
    w_uq_ext = jnp.concatenate([w_qn] + w_rope, axis=1)
    return w_in_ext.astype(BF16), w_uq_ext.astype(BF16)


def _mla_project(x, mod, g_norm4, layer, w_in, g_q, g_kv, w_uq, w_uk, w_uv, pos_rows, rows_per_pos_table, absorb):
    rows, d = x.shape
    q_lora, kv_lora = g_q.shape[0], g_kv.shape[0]
    rope = w_in.shape[1] - q_lora - kv_lora
    n_heads, nope = w_uk.shape[1], w_uk.shape[2]
    v_dim = w_uv.shape[2]
    w_in_ext, w_uq_ext = _mla_weights(w_in, w_uq, q_lora, kv_lora, rope, nope, absorb)
    cos_k, sin_k = _rope_tables(pos_rows, rope, LANES, 1)
    period = rows_per_pos_table
    tm = _tile(period, ROW_TILE)
    tiles_per_period = period // tm
    row = lambda w: pl.BlockSpec((tm, w), lambda i: (i, 0))
    tab = lambda w: pl.BlockSpec((tm, w), lambda i: (i % tiles_per_period, 0))
    full = lambda a: _resident(a.shape, lambda i: (0,) * a.ndim)
    hn = n_heads * nope
    in_specs = [
        row(d), _mod_spec(mod, layer, rows, tm),
        _resident((None, None, 1, d), lambda i: (layer, 1, 0, 0)),
        full(w_in_ext), _resident((1, q_lora), lambda i: (0, 0)), _resident((1, kv_lora), lambda i: (0, 0)),
        full(w_uq_ext), tab(LANES), tab(LANES),
    ]
    args = [x, mod, g_norm4, w_in_ext, g_q.reshape(1, q_lora), g_kv.reshape(1, kv_lora), w_uq_ext, cos_k, sin_k]
    if absorb:
        hr = n_heads * rope
        cos_q, sin_q = _rope_tables(pos_rows, rope, rope, n_heads)
        w_ukt = jnp.transpose(w_uk, (1, 2, 0)).astype(BF16)
        in_specs += [tab(hr), tab(hr), full(w_ukt)]
        args += [cos_q, sin_q, w_ukt]
        outs = [(n_heads * kv_lora, BF16), (hr, BF16)]
    else:
        qk_w = n_heads * (nope + LANES)
        hv = n_heads * v_dim
        w_uk2 = w_uk.reshape(kv_lora, hn).astype(BF16)
        w_uvt = w_uv.reshape(kv_lora, hv).T.astype(BF16)
        in_specs += [full(w_uk2), full(w_uvt)]
        args += [w_uk2, w_uvt]
        outs = [(qk_w, BF16), (qk_w, BF16)]
    out_specs = [row(w) for w, _ in outs]
    out_shape = [jax.ShapeDtypeStruct((rows, w), dt) for w, dt in outs]
    if not absorb:
        out_specs.append(pl.BlockSpec((None, None, hv, tm), lambda i: (i // tiles_per_period, i % tiles_per_period, 0, 0)))
        out_shape.append(jax.ShapeDtypeStruct((rows // period, tiles_per_period, hv, tm), BF16))
    out_specs += [row(kv_lora), row(rope)]
    out_shape += [jax.ShapeDtypeStruct((rows, kv_lora), F32), jax.ShapeDtypeStruct((rows, rope), F32)]
    return pl.pallas_call(
        functools.partial(_mla_proj_kernel, dims=(q_lora, kv_lora, rope, n_heads, nope), absorb=absorb),
        grid=(rows // tm,),
        in_specs=in_specs,
        out_specs=out_specs,
        out_shape=out_shape,
        compiler_params=_params("arbitrary"),
    )(*args)


def _attn_kernel(q_ref, k_ref, vt_ref, x_ref, mod_ref, wo_ref, o_ref, oh_ref,
                 *, n_heads, heads_per_step, v_dim, scale):
    tq = q_ref.shape[0]
    tk = vt_ref.shape[-1]
    assert tq == tk and tk % (2 * LANES) == 0
    half = tk // 2
    w = q_ref.shape[1] // n_heads
    diag = pl.program_id(1)

    log2e_scale = scale * LOG2_E

    def update(c, carry, heads, key_lo, n_keys, q_lo, masked):
        rows = pl.ds(pl.multiple_of(c * tk, tk) + key_lo, n_keys)
        scores = []
        for i in heads:
            s = _dot_nt(k_ref[rows, i * w:(i + 1) * w], q_ref[q_lo:, i * w:(i + 1) * w])
            if masked:
                key = key_lo + lax.broadcasted_iota(jnp.int32, (n_keys, 1), 0)
                query = q_lo + lax.broadcasted_iota(jnp.int32, (1, tq - q_lo), 1)
                s = jnp.where(query >= key, s, -jnp.inf)
            scores.append(s)
        out = []
        for i, s, (m, l, acc) in zip(heads, scores, carry):
            m_new = jnp.maximum(m[:, q_lo:], jnp.max(s, axis=0, keepdims=True) * log2e_scale)
            alpha = jnp.exp2(m[:, q_lo:] - m_new)
            p = jnp.exp2(s * log2e_scale - m_new)
            l_new = alpha * l[:, q_lo:] + jnp.sum(p, axis=0, keepdims=True)
            values = vt_ref[c, i * v_dim:(i + 1) * v_dim, key_lo:key_lo + n_keys]
            acc_new = alpha * acc[:, q_lo:] + _dot(values, p.astype(BF16))
            if q_lo:
                m_new, l_new, acc_new = (jnp.concatenate([old[:, :q_lo], new], axis=1)
                                         for old, new in ((m, m_new), (l, l_new), (acc, acc_new)))
            out.append((m_new, l_new, acc_new))
        return tuple(out)

    init = (jnp.full((1, tq), NEG_BIG, F32), jnp.zeros((1, tq), F32), jnp.zeros((v_dim, tq), F32))
    for g in range(0, n_heads, heads_per_step):
        heads = tuple(range(g, g + heads_per_step))
        full_block = functools.partial(update, heads=heads, key_lo=0, n_keys=tk, q_lo=0, masked=False)
        carry = lax.fori_loop(0, diag, full_block, (init,) * heads_per_step)
        carry = update(diag, carry, heads, 0, half, 0, True)
        carry = update(diag, carry, heads, half, half, half, True)
        for i, (_, l, acc) in zip(heads, carry):
            oh_ref[:, i * v_dim:(i + 1) * v_dim] = (acc / l).T.astype(BF16)
    o_ref[...] = x_ref[...] + _mod(mod_ref, MIXER_GATE) * _dot(oh_ref[...], wo_ref[...])


def _attn_prompt(q, k, vt, x, mod, w_o, layer, j, n_seq, n_heads, scale):
    rows, d = x.shape
    seq = rows // n_seq
    _, n_kv, hv, tk = vt.shape
    assert n_kv * tk == seq
    tq = _tile(seq, ROW_TILE)
    r3 = lambda a: a.reshape(n_seq, seq, a.shape[-1])
    q_spec = lambda w: pl.BlockSpec((None, tq, w), lambda b, i: (b, i, 0))
    out = pl.pallas_call(
        functools.partial(_attn_kernel, n_heads=n_heads, heads_per_step=_tile(n_heads, ATTN_HEADS_PER_STEP),
                          v_dim=hv // n_heads, scale=scale),
        grid=(n_seq, seq // tq),
        in_specs=[
            q_spec(q.shape[-1]),
            pl.BlockSpec((None, seq, k.shape[-1]), lambda b, i: (b, 0, 0)),
            pl.BlockSpec((None, n_kv, hv, tk), lambda b, i: (b, 0, 0, 0)),
            q_spec(d),
            pl.BlockSpec((None, None, 1, mod.shape[-1]), lambda b, i: (layer, b, 0, 0)),
            _resident((None, hv, d), lambda b, i: (j, 0, 0)),
        ],
        out_specs=q_spec(d),
        out_shape=jax.ShapeDtypeStruct((n_seq, seq, d), F32),
        scratch_shapes=[pltpu.VMEM((tq, hv), BF16)],
        compiler_params=_params("arbitrary", "arbitrary"),
    )(r3(q), r3(k), vt, r3(x), mod, w_o)
    return out.reshape(rows, d)


def _page_copies(pt_ref, lat_hbm, krt_hbm, lat_buf, krt_buf, sem, layer, seq, chunk, slot):
    pages_per_chunk = lat_buf.shape[1]
    out = []
    for k in range(pages_per_chunk):
        pg = pt_ref[seq, chunk * pages_per_chunk + k]
        out.append(pltpu.make_async_copy(lat_hbm.at[layer, pg], lat_buf.at[slot, k], sem.at[0, slot]))
        out.append(pltpu.make_async_copy(krt_hbm.at[layer, pg], krt_buf.at[slot, k], sem.at[1, slot]))
    return out


def _start_pages(copies):
    for n, cp in enumerate(copies):
        cp.start(priority=(n // 2) % 2)


def _wait_slot(lat_hbm, krt_hbm, lat_buf, krt_buf, sem, layer, slot):
    pages_per_chunk = lat_buf.shape[1]
    pltpu.make_async_copy(lat_hbm.at[layer, pl.ds(0, pages_per_chunk)], lat_buf.at[slot], sem.at[0, slot]).wait()
    pltpu.make_async_copy(krt_hbm.at[layer, pl.ds(0, pages_per_chunk)], krt_buf.at[slot], sem.at[1, slot]).wait()


def _softmax_update(state, s, values):
    m, l, acc = state
    m_new = jnp.maximum(m, jnp.max(s, axis=-1, keepdims=True))
    alpha = jnp.exp(m - m_new)
    p = jnp.exp(s - m_new)
    l = alpha * l + jnp.sum(p, axis=-1, keepdims=True)
    acc = alpha * acc + _dot(p.astype(BF16), values)
    return m_new, l, acc


def _decode_init(n_rows, kv_lora, n_split):
    init = (jnp.full((n_rows, 1), NEG_BIG, F32), jnp.zeros((n_rows, 1), F32), jnp.zeros((n_rows, kv_lora), F32))
    return (init,) * n_split


def _decode_part(k, n_split, ql, qr, lat_buf, krt_buf, slot, scale):
    per_part = lat_buf.shape[1] // n_split
    pages = range(k * per_part, (k + 1) * per_part)
    lat = jnp.concatenate([lat_buf[slot, pg].astype(BF16) for pg in pages], axis=0)
    krt = jnp.concatenate([krt_buf[slot, pg].astype(BF16) for pg in pages], axis=1)
    return lat, (_dot_nt(ql, lat) + _dot(qr, krt)) * scale


def _decode_chunk(states, ql, qr, lat_buf, krt_buf, slot, scale):
    parts = [_decode_part(k, len(states), ql, qr, lat_buf, krt_buf, slot, scale) for k in range(len(states))]
    return tuple(_softmax_update(state, s, lat) for state, (lat, s) in zip(states, parts))


def _decode_finish(states, ql, qr, cn, kn, n_heads, n_new, scale):
    m = functools.reduce(jnp.maximum, [st[0] for st in states])
    weights = [jnp.exp(st[0] - m) for st in states]
    merged = (m, sum(w * st[1] for w, st in zip(weights, states)), sum(w * st[2] for w, st in zip(weights, states)))
    s_new = (_dot_nt(ql, cn) + _dot_nt(qr, kn)) * scale
    t = lax.broadcasted_iota(jnp.int32, (ql.shape[0], 1), 0) // n_heads
    u = lax.broadcasted_iota(jnp.int32, (1, cn.shape[0]), 1)
    s_new = jnp.where((u <= t) & (u < n_new), s_new, -jnp.inf)
    _, l, acc = _softmax_update(merged, s_new, cn)
    return acc / l


def _decode_kernel(pt_ref, ql_ref, qr_ref, cn_ref, kn_ref, lat_hbm, krt_hbm, o_ref, lat_buf, krt_buf, sem,
                   *, layer, n_chunks, n_heads, n_new, n_split, scale):
    b = pl.program_id(0)
    n_b = pl.num_programs(0)
    copies = functools.partial(_page_copies, pt_ref, lat_hbm, krt_hbm, lat_buf, krt_buf, sem, layer)

    @pl.when(b == 0)
    def _():
        _start_pages(copies(0, 0, 0))

    ql = ql_ref[...]
    qr = qr_ref[...]

    def step(c, states):
        slot = (b * n_chunks + c) % 2
        last = c + 1 == n_chunks
        nb = jnp.where(last, b + 1, b)
        nc = jnp.where(last, 0, c + 1)

        @pl.when(nb < n_b)
        def _():
            _start_pages(copies(nb, nc, 1 - slot))

        _wait_slot(lat_hbm, krt_hbm, lat_buf, krt_buf, sem, layer, slot)
        return _decode_chunk(states, ql, qr, lat_buf, krt_buf, slot, scale)

    states = lax.fori_loop(0, n_chunks, step, _decode_init(ql.shape[0], lat_buf.shape[-1], n_split))
    o_ref[...] = _decode_finish(states, ql, qr, cn_ref[...], kn_ref[...], n_heads, n_new, scale)


def _decode_setup(dec):
    page_table, q_lat, q_rope, ckv_new, kr_new, cache_lat, cache_krt, cache_layer, n_heads, scale = dec
    _, n_rows, kv_lora = q_lat.shape
    rope = q_rope.shape[-1]
    n_new = n_rows // n_heads
    page = cache_lat.shape[2]
    n_pages = page_table.shape[1]
    pages_per_chunk = _tile(n_pages, PAGES_PER_CHUNK)
    assert n_new <= LANES
    pad_keys = lambda a: jnp.pad(a.astype(BF16), ((0, 0), (0, LANES - n_new), (0, 0)))
    operands = [q_lat, q_rope, pad_keys(ckv_new), pad_keys(kr_new), cache_lat, cache_krt]
    block_dims = [(n_rows, kv_lora), (n_rows, rope), (LANES, kv_lora), (LANES, rope)]
    scratch = [
        pltpu.VMEM((2, pages_per_chunk, page, kv_lora), F32),
        pltpu.VMEM((2, pages_per_chunk, rope, page), F32),
        pltpu.SemaphoreType.DMA((2, 2)),
    ]
    static = dict(layer=cache_layer, n_chunks=n_pages // pages_per_chunk, n_heads=n_heads, n_new=n_new,
                  n_split=_tile(pages_per_chunk, KEY_PARTS), scale=scale)
    return page_table, operands, block_dims, scratch, static


def _decode_attn(dec):
    page_table, operands, block_dims, scratch, static = _decode_setup(dec)
    n_b, n_rows, kv_lora = operands[0].shape
    per_b = lambda dims: pl.BlockSpec((None,) + dims, lambda b, pt: (b, 0, 0))
    any_space = pl.BlockSpec(memory_space=pl.ANY)
    return pl.pallas_call(
        functools.partial(_decode_kernel, **static),
        grid_spec=pltpu.PrefetchScalarGridSpec(
            num_scalar_prefetch=1,
            grid=(n_b,),
            in_specs=[per_b(dims) for dims in block_dims] + [any_space, any_space],
            out_specs=per_b((n_rows, kv_lora)),
            scratch_shapes=scratch,
        ),
        out_shape=jax.ShapeDtypeStruct((n_b, n_rows, kv_lora), F32),
        compiler_params=_params("arbitrary"),
    )(page_table, *operands)


def _ffn_decode_kernel(pt_ref, *refs, s, tf, final_norm, seq0, layer, n_chunks, n_heads, n_new, n_split, scale):
    if final_norm:
        x_ref, mod_ref, g_ref, wgu_ref, wd_ref, gf_ref, *refs = refs
    else:
        x_ref, mod_ref, g_ref, wgu_ref, wd_ref, *refs = refs
        gf_ref = None
    ql_ref, qr_ref, cn_ref, kn_ref, lat_hbm, krt_hbm, o_ref, ol_ref, h_ref, a_ref, lat_buf, krt_buf, sem = refs
    i = pl.program_id(0)
    n = pl.num_programs(0)
    seqs = ql_ref.shape[0]
    first = seq0 + i * seqs
    copies = functools.partial(_page_copies, pt_ref, lat_hbm, krt_hbm, lat_buf, krt_buf, sem, layer)
    items = [(q, c) for q in range(seqs) for c in range(n_chunks)]
    assert len(items) % 2 == 0

    @pl.when(i == 0)
    def _():
        _start_pages(copies(first, 0, 0))

    states = {}

    def chunk_step(w):
        q, c = items[w]
        if w + 1 < len(items):
            _start_pages(copies(first + items[w + 1][0], items[w + 1][1], (w + 1) % 2))
        else:
            _start_pages(copies(jnp.where(i + 1 < n, first + seqs, seq0), 0, 0))
        _wait_slot(lat_hbm, krt_hbm, lat_buf, krt_buf, sem, layer, w % 2)
        ql, qr = ql_ref[q], qr_ref[q]
        prev = _decode_init(ql.shape[0], lat_buf.shape[-1], n_split) if c == 0 else states[q]
        states[q] = _decode_chunk(prev, ql, qr, lat_buf, krt_buf, w % 2, scale)
        if c == n_chunks - 1:
            ol_ref[q] = _decode_finish(states[q], ql, qr, cn_ref[q], kn_ref[q], n_heads, n_new, scale)

    n_f = wd_ref.shape[0] // tf
    after = {}
    for w in range(len(items)):
        after.setdefault(2 * min(n_f - 1, (2 * w + 1) * n_f // (2 * len(items))) + 1, []).append(w)

    def after_dot(gap):
        for w in after.get(gap, ()):
            chunk_step(w)

    _ffn_tile(x_ref, mod_ref, g_ref, wgu_ref, wd_ref, gf_ref, o_ref, h_ref, a_ref, s, tf, after_dot)

    @pl.when(i == n - 1)
    def _():
        _wait_slot(lat_hbm, krt_hbm, lat_buf, krt_buf, sem, layer, 0)


def _decode_fits_ffn_calls(dec, n_calls, x_rows, mod):
    page_table, q_lat = dec[0], dec[1]
    n_steps = x_rows // _tile(_rows_per_mod(mod, x_rows), ROW_TILE)
    n_chunks = page_table.shape[1] // _tile(page_table.shape[1], PAGES_PER_CHUNK)
    n_seq = q_lat.shape[0]
    return n_seq % (n_calls * n_steps) == 0 and (n_seq // (n_calls * n_steps) * n_chunks) % 2 == 0


def _ffn_with_decode(x, mod, g_norm4, w_gu, w_down, layer, k, s, g_final, dec, seq0, n_seq):
    rows, d = x.shape
    f = w_down.shape[2]
    tm, tf, in_specs, args = _ffn_operands(x, mod, g_norm4, w_gu, w_down, layer, k, s, g_final)
    page_table, operands, block_dims, scratch, static = _decode_setup(dec)
    n_steps = rows // tm
    seqs = n_seq // n_steps
    assert seqs * n_steps == n_seq and seq0 % seqs == 0
    per_step = lambda dims, blk0: pl.BlockSpec((seqs,) + dims, lambda i, pt: (blk0 + i, 0, 0))
    any_space = pl.BlockSpec(memory_space=pl.ANY)
    n_rows, kv_lora = block_dims[0]
    return pl.pallas_call(
        functools.partial(_ffn_decode_kernel, s=s, tf=tf, final_norm=g_final is not None, seq0=seq0, **static),
        grid_spec=pltpu.PrefetchScalarGridSpec(
            num_scalar_prefetch=1,
            grid=(n_steps,),
            in_specs=in_specs + [per_step(dims, seq0 // seqs) for dims in block_dims] + [any_space, any_space],
            out_specs=[pl.BlockSpec((tm, d), lambda i, pt: (i, 0)), per_step((n_rows, kv_lora), 0)],
            scratch_shapes=[pltpu.VMEM((tm, d), BF16), pltpu.VMEM((tm, f), BF16)] + scratch,
        ),
        out_shape=[jax.ShapeDtypeStruct((rows, d), F32), jax.ShapeDtypeStruct((n_seq, n_rows, kv_lora), F32)],
        compiler_params=_params("arbitrary"),
    )(page_table, *args, *operands)


def _decode_out_kernel(ol_ref, x_ref, mod_ref, wuv_ref, wo_ref, o_ref, oh_ref, *, n_heads, kv_lora, v_dim):
    for i in range(n_heads):
        o_i = ol_ref[:, i * kv_lora:(i + 1) * kv_lora].astype(BF16)
        oh_ref[:, i * v_dim:(i + 1) * v_dim] = _dot(o_i, wuv_ref[i]).astype(BF16)
    o_ref[...] = x_ref[...] + _mod(mod_ref, MIXER_GATE) * _dot(oh_ref[...], wo_ref[...])


def _decode_out(o_lat, x, mod, w_uv, w_o, layer, j):
    rows, d = x.shape
    kv_lora, n_heads, v_dim = w_uv.shape
    w_uv_h = jnp.transpose(w_uv, (1, 0, 2)).astype(BF16)
    tm = _tile(rows, ROW_TILE)
    row = lambda w: pl.BlockSpec((tm, w), lambda i: (i, 0))
    return pl.pallas_call(
        functools.partial(_decode_out_kernel, n_heads=n_heads, kv_lora=kv_lora, v_dim=v_dim),
        grid=(rows // tm,),
        in_specs=[
            row(n_heads * kv_lora), row(d), _mod_spec(mod, layer, rows, tm),
            _resident(w_uv_h.shape, lambda i: (0, 0, 0)),
            _resident((None, n_heads * v_dim, d), lambda i: (j, 0, 0)),
        ],
        out_specs=row(d),
        out_shape=jax.ShapeDtypeStruct((rows, d), F32),
        scratch_shapes=[pltpu.VMEM((tm, n_heads * v_dim), BF16)],
        compiler_params=_params("arbitrary"),
    )(o_lat, x, mod, w_uv_h, w_o)


def kernel(x_prompt, x_sample, c_prompt, c_sample, state_conv, cache_kv_latent, cache_k_rope, page_table, w_ada, b_ada, g_norm, w_ffn_gu, w_ffn_down, w_conv_in, w_conv_k, w_conv_out, w_mla_in, g_q_norm, g_kv_norm, w_uq, w_uk, w_uv, w_mla_out, g_final):
    n_p, seq_p, d = x_prompt.shape
    n_s, seq_s, _ = x_sample.shape
    depth = w_ada.shape[0]
    past = page_table.shape[1] * cache_kv_latent.shape[2]
    n_heads, qk_dim = w_uq.shape[2], w_uq.shape[3]
    v_dim = w_uv.shape[3]
    kv_lora = g_kv_norm.shape[1]
    scale = qk_dim ** -0.5

    xp = x_prompt.reshape(n_p * seq_p, d)
    xs = x_sample.reshape(n_s * seq_s, d)
    mp, ms = _adaln(c_prompt, jnp.repeat(c_sample, seq_s, axis=0), w_ada, b_ada)
    mp = mp.reshape(depth, n_p, 1, N_MOD * d)
    g_norm4 = g_norm.reshape(depth, N_SUB, 1, d)
    w_gu = w_ffn_gu.astype(BF16)
    w_down = w_ffn_down.astype(BF16)
    w_cin = w_conv_in.astype(BF16)
    w_cout = w_conv_out.astype(BF16)
    w_mo = w_mla_out.astype(BF16)
    cache_krt = jnp.swapaxes(cache_k_rope, 2, 3)

    conv_p, conv_s, lat_p, kr_p, lat_s, kr_s = [], [], [], [], [], []
    for i in range(depth):
        gf = g_final if i == depth - 1 else None
        ffn_a = lambda x, m: _ffn(x, m, g_norm4, w_gu, w_down, i, 0, 0)
        ffn_b = lambda x, m: _ffn(x, m, g_norm4, w_gu, w_down, i, 1, 2, gf)
        j = i // N_MIXERS
        if i % N_MIXERS == 0:
            xp, xs = ffn_a(xp, mp), ffn_a(xs, ms)
            xp, st_p = _conv_prompt(xp, mp, g_norm4, w_cin, w_conv_k, w_cout, i, j, n_p)
            xs, st_s = _conv_sample(xs, ms, g_norm4, w_cin, w_conv_k, w_cout, state_conv[j], i, j)
            conv_p.append(st_p)
            conv_s.append(st_s)
            xp, xs = ffn_b(xp, mp), ffn_b(xs, ms)
            continue
        proj = (w_mla_in[j], g_q_norm[j], g_kv_norm[j], w_uq[j], w_uk[j], w_uv[j])
        xs = ffn_a(xs, ms)
        pos_s = jnp.tile(past + jnp.arange(seq_s, dtype=jnp.int32), n_s)
        ql, qrs, ckv_s, k_s = _mla_project(xs, ms, g_norm4, i, *proj, pos_s, n_s * seq_s, absorb=True)
        dec = (page_table, ql.reshape(n_s, seq_s * n_heads, kv_lora), qrs.reshape(n_s, seq_s * n_heads, -1),
               ckv_s.reshape(n_s, seq_s, -1), k_s.reshape(n_s, seq_s, -1), cache_kv_latent, cache_krt, j, n_heads, scale)
        fused = _decode_fits_ffn_calls(dec, 2, xp.shape[0], mp)
        half = n_s // 2
        if fused:
            xp, o_lat_a = _ffn_with_decode(xp, mp, g_norm4, w_gu, w_down, i, 0, 0, None, dec, 0, half)
        else:
            xp = ffn_a(xp, mp)
        q, k, vt, ckv_p, k_p = _mla_project(
            xp, mp, g_norm4, i, *proj, jnp.arange(seq_p, dtype=jnp.int32), seq_p, absorb=False)
        xp = _attn_prompt(q, k, vt, xp, mp, w_mo, i, j, n_p, n_heads, scale)
        if fused:
            xp, o_lat_b = _ffn_with_decode(xp, mp, g_norm4, w_gu, w_down, i, 1, 2, gf, dec, half, half)
            o_lat = jnp.concatenate([o_lat_a, o_lat_b], axis=0)
        else:
            xp = ffn_b(xp, mp)
            o_lat = _decode_attn(dec)
        xs = _decode_out(o_lat.reshape(n_s * seq_s, n_heads * kv_lora), xs, ms, w_uv[j], w_mo, i, j)
        xs = ffn_b(xs, ms)
        lat_p.append(ckv_p.reshape(n_p, seq_p, -1))
        kr_p.append(k_p.reshape(n_p, seq_p, -1))
        lat_s.append(ckv_s.reshape(n_s, seq_s, -1))
        kr_s.append(k_s.reshape(n_s, seq_s, -1))
    return (xp.reshape(n_p, seq_p, d), xs.reshape(n_s, seq_s, d), jnp.stack(conv_p), jnp.stack(conv_s),
            jnp.stack(lat_p), jnp.stack(kr_p), jnp.stack(lat_s), jnp.stack(kr_s))
```
